```python
import jax, jax.numpy as jnp
from jax import lax
import numpy as np

D_MODEL = 1024
BATCH = 32
SEQ = 256
DEPTH = 1
DEC_BATCH = 8
DEC_SEQ = 1024
PAST_LEN = 512

GRID_W = 64
D_MIX = D_MODEL
GLA_HEADS = 4
GLA_DV = D_MIX // 2
GLA_DK = GLA_DV // 2
GLA_HK = GLA_DK // GLA_HEADS
GLA_HV = GLA_DV // GLA_HEADS
GLA_RANK = 16
GLA_NORMALIZER = 16.0
CHUNK = 64
CONV_CH = D_MIX - GLA_DV
CONV_K = 31
CONV_PAD = CONV_K // 2
PEER_HEADS = 8
PEER_NKEYS = 128
PEER_N = PEER_NKEYS * PEER_NKEYS
PEER_DQ = 256
PEER_TOPK = 16
PEER_BLOCK = 128
EPS = 1e-6
COL_SIZES = (GLA_DK, GLA_DK, GLA_DV, GLA_DV, GLA_RANK, GLA_RANK, CONV_CH, CONV_CH)
D_IN = sum(COL_SIZES)

kernel_name = "hybrid_gla_conformer_peer_diffusion_step"


def _rmsnorm(x, g):
    xf = x.astype(jnp.float32)
    y = xf * lax.rsqrt(jnp.mean(xf * xf, axis=-1, keepdims=True) + EPS)
    return (y * g.astype(jnp.float32)).astype(x.dtype)


def _layernorm(x, g, b):
    xf = x.astype(jnp.float32)
    mu = jnp.mean(xf, axis=-1, keepdims=True)
    var = jnp.mean(jnp.square(xf - mu), axis=-1, keepdims=True)
    y = (xf - mu) * lax.rsqrt(var + EPS)
    return (y * g.astype(jnp.float32) + b.astype(jnp.float32)).astype(x.dtype)


def _gla_scan(q, k, v, gk, s0):
    B, L, H, _ = q.shape
    n = L // CHUNK
    def to_chunks(t):
        return jnp.moveaxis(t.astype(jnp.float32).reshape(B, n, CHUNK, H, t.shape[-1]), 1, 0)
    mask = jnp.tril(jnp.ones((CHUNK, CHUNK), bool))[None, :, :, None, None]
    def step(S, inp):
        qc, kc, vc, gc = inp
        b = jnp.cumsum(gc, axis=1)
        diff = b[:, :, None] - b[:, None, :]
        decay = jnp.exp(jnp.where(mask, diff, -jnp.inf))
        scores = jnp.einsum('bihk,bijhk,bjhk->bhij', qc, decay, kc)
        o_intra = jnp.einsum('bhij,bjhv->bihv', scores, vc)
        o_inter = jnp.einsum('bihk,bhkv->bihv', qc * jnp.exp(b), S)
        k_dec = kc * jnp.exp(b[:, -1:] - b)
        S_new = jnp.exp(b[:, -1])[..., None] * S + jnp.einsum('bjhk,bjhv->bhkv', k_dec, vc)
        return S_new, o_intra + o_inter
    S_fin, o = lax.scan(step, s0.astype(jnp.float32), (to_chunks(q), to_chunks(k), to_chunks(v), to_chunks(gk)))
    o = jnp.moveaxis(o, 0, 1).reshape(B, L, H, v.shape[-1])
    return o, S_fin


def _gla_bidir(q, k, v, gf, gb, sf0, sb0):
    of, sf = _gla_scan(q, k, v, gf, sf0)
    fl = lambda t: jnp.flip(t, axis=1)
    ob, sb = _gla_scan(fl(q), fl(k), fl(v), fl(gb), sb0)
    return of + fl(ob), sf, sb


def _dwconv_seq(x, w):
    C = x.shape[-1]
    return lax.conv_general_dilated(x, w[:, None, :].astype(x.dtype), (1,), [(CONV_PAD, CONV_PAD)],
                                    dimension_numbers=('NWC', 'WIO', 'NWC'), feature_group_count=C)


def _dwconv_grid(x, w):
    B, L, C = x.shape
    rows = L // GRID_W
    xg = x.reshape(B, rows, GRID_W, C)
    h = C // 2
    wd = w.astype(x.dtype)
    dn = ('NHWC', 'HWIO', 'NHWC')
    xh = lax.conv_general_dilated(xg[..., :h], wd[None, :, None, :h], (1, 1), [(0, 0), (CONV_PAD, CONV_PAD)],
                                  dimension_numbers=dn, feature_group_count=h)
    xv = lax.conv_general_dilated(xg[..., h:], wd[:, None, None, h:], (1, 1), [(CONV_PAD, CONV_PAD), (0, 0)],
                                  dimension_numbers=dn, feature_group_count=C - h)
    return jnp.concatenate([xh, xv], axis=-1).reshape(B, L, C)


def _mixer(h, latent, sf0, sb0, w_in, w_af_up, b_af, w_ab_up, b_ab, gla_norm_g,
           conv_dw_w, conv_dw_b, conv_ln_g, conv_ln_b, conv_pw2, w_out):
    B, L, _ = h.shape
    offs = np.cumsum(COL_SIZES)[:-1].tolist()
    q, k, v, g, af, ab, ca, cb = jnp.split(h @ w_in, offs, axis=-1)
    heads = lambda t, d: t.reshape(B, L, GLA_HEADS, d)
    gf = jax.nn.log_sigmoid((af @ w_af_up + b_af).astype(jnp.float32)) / GLA_NORMALIZER
    gb = jax.nn.log_sigmoid((ab @ w_ab_up + b_ab).astype(jnp.float32)) / GLA_NORMALIZER
    o, sf, sb = _gla_bidir(heads(q * (GLA_HK ** -0.5), GLA_HK), heads(k, GLA_HK), heads(v, GLA_HV),
                           heads(gf, GLA_HK), heads(gb, GLA_HK), sf0, sb0)
    o = o * lax.rsqrt(jnp.mean(o * o, axis=-1, keepdims=True) + EPS)
    o = (o.reshape(B, L, GLA_DV) * gla_norm_g.astype(jnp.float32)).astype(h.dtype) * jax.nn.silu(g)
    u = ca * jax.nn.sigmoid(cb)
    u = (_dwconv_grid(u, conv_dw_w) if latent else _dwconv_seq(u, conv_dw_w)) + conv_dw_b
    u = jax.nn.silu(_layernorm(u, conv_ln_g, conv_ln_b)) @ conv_pw2
    out = jnp.concatenate([o, u], axis=-1) @ w_out
    return out, sf.astype(h.dtype), sb.astype(h.dtype)


def _peer(h, w_q, sub_keys, u_tab, v_tab):
    B, L, D = h.shape
    T = B * L
    xt = h.reshape(T, D)
    qf = (xt @ w_q).astype(jnp.float32).reshape(T, PEER_HEADS, 2, PEER_DQ // 2)
    s = jnp.einsum('thpd,hpnd->thpn', qf, sub_keys.astype(jnp.float32))
    sv, si = lax.top_k(s, PEER_TOPK)
    cand = (sv[:, :, 0, :, None] + sv[:, :, 1, None, :]).reshape(T, PEER_HEADS, PEER_TOPK * PEER_TOPK)
    cidx = (si[:, :, 0, :, None] * PEER_NKEYS + si[:, :, 1, None, :]).reshape(T, PEER_HEADS, PEER_TOPK * PEER_TOPK)
    top_s, pos = lax.top_k(cand, PEER_TOPK)
    idx = jnp.take_along_axis(cidx, pos, axis=-1)
    gate = jax.nn.softmax(top_s, axis=-1).astype(h.dtype)
    nb = T // PEER_BLOCK
    def block(args):
        xb, ib, gb = args
        act = jax.nn.gelu(jnp.einsum('td,thkd->thk', xb, u_tab[ib]), approximate=False)
        return jnp.einsum('thk,thkd->td', gb * act, v_tab[ib])
    out = lax.map(block, (xt.reshape(nb, PEER_BLOCK, D),
                          idx.reshape(nb, PEER_BLOCK, PEER_HEADS, PEER_TOPK),
                          gate.reshape(nb, PEER_BLOCK, PEER_HEADS, PEER_TOPK)))
    return out.reshape(B, L, D)


def _layer(x, mod, latent, sf0, sb0, norm1_g, norm2_g, mw, pw):
    shift1, scale1, gate1, shift2, scale2, gate2 = jnp.split(mod, 6, axis=-1)
    h = _rmsnorm(x, norm1_g) * (1.0 + scale1) + shift1
    m, sf, sb = _mixer(h, latent, sf0, sb0, *mw)
    x = x + gate1 * m
    h = _rmsnorm(x, norm2_g) * (1.0 + scale2) + shift2
    x = x + gate2 * _peer(h, *pw)
    return x, sf, sb


def setup_inputs(seed: int = 0) -> dict:
    key = jax.random.key(seed)
    ks = jax.random.split(key, 32)
    nrm = lambda i, shape, s: jax.random.normal(ks[i], shape, jnp.float32) * s
    D = D_MODEL
    st_shape = (DEC_BATCH, DEPTH, GLA_HEADS, GLA_HK, GLA_HV)
    return {
        'x_prompt': nrm(0, (BATCH, SEQ, D), 1.0),
        'x_sample': nrm(1, (DEC_BATCH, DEC_SEQ, D), 1.0),
        'c': nrm(2, (DEC_BATCH, D), 1.0),
        'state_gla_fwd': nrm(3, st_shape, 0.5),
        'state_gla_bwd': nrm(4, st_shape, 0.5),
        'c_ctx': nrm(5, (D,), 1.0),
        'norm1_g': 1.0 + nrm(6, (DEPTH, D), 0.02),
        'w_mod': nrm(7, (DEPTH, D, 6 * D), 0.5 * D ** -0.5),
        'b_mod': nrm(8, (DEPTH, 6 * D), 0.02),
        'w_in': nrm(9, (DEPTH, D, D_IN), D ** -0.5),
        'w_af_up': nrm(10, (DEPTH, GLA_RANK, GLA_DK), GLA_RANK ** -0.5),
        'b_af': nrm(11, (DEPTH, GLA_DK), 0.1) + 1.0,
        'w_ab_up': nrm(12, (DEPTH, GLA_RANK, GLA_DK), GLA_RANK ** -0.5),
        'b_ab': nrm(13, (DEPTH, GLA_DK), 0.1) + 1.0,
        'gla_norm_g': 1.0 + nrm(14, (DEPTH, GLA_DV), 0.02),
        'conv_dw_w': nrm(15, (DEPTH, CONV_K, CONV_CH), CONV_K ** -0.5),
        'conv_dw_b': nrm(16, (DEPTH, CONV_CH), 0.02),
        'conv_ln_g': 1.0 + nrm(17, (DEPTH, CONV_CH), 0.02),
        'conv_ln_b': nrm(18, (DEPTH, CONV_CH), 0.02),
        'conv_pw2': nrm(19, (DEPTH, CONV_CH, CONV_CH), CONV_CH ** -0.5),
        'w_out': nrm(20, (DEPTH, D_MIX, D), D_MIX ** -0.5),
        'norm2_g': 1.0 + nrm(21, (DEPTH, D), 0.02),
        'peer_wq': nrm(22, (DEPTH, D, PEER_HEADS * PEER_DQ), D ** -0.5),
        'peer_subkeys': nrm(23, (DEPTH, PEER_HEADS, 2, PEER_NKEYS, PEER_DQ // 2), (PEER_DQ // 2) ** -0.5),
        'peer_u': nrm(24, (DEPTH, PEER_N, D), D ** -0.5),
        'peer_v': nrm(25, (DEPTH, PEER_N, D), PEER_HEADS ** -0.5),
        'final_norm_g': 1.0 + nrm(26, (D,), 0.02),
    }


def reference(x_prompt, x_sample, c, state_gla_fwd, state_gla_bwd, c_ctx, norm1_g, w_mod, b_mod,
              w_in, w_af_up, b_af, w_ab_up, b_ab, gla_norm_g, conv_dw_w, conv_dw_b, conv_ln_g,
              conv_ln_b, conv_pw2, w_out, norm2_g, peer_wq, peer_subkeys, peer_u, peer_v, final_norm_g):
    yp, ys = x_prompt, x_sample
    zero_state = jnp.zeros((x_prompt.shape[0], GLA_HEADS, GLA_HK, GLA_HV), jnp.float32)
    new_f, new_b = [], []
    for l in range(DEPTH):
        mw = (w_in[l], w_af_up[l], b_af[l], w_ab_up[l], b_ab[l], gla_norm_g[l], conv_dw_w[l], conv_dw_b[l],
              conv_ln_g[l], conv_ln_b[l], conv_pw2[l], w_out[l])
        pw = (peer_wq[l], peer_subkeys[l], peer_u[l], peer_v[l])
        mod_ctx = jax.nn.silu(c_ctx) @ w_mod[l] + b_mod[l]
        mod_lat = (jax.nn.silu(c) @ w_mod[l] + b_mod[l])[:, None, :]
        yp, sf, sb = _layer(yp, mod_ctx, False, zero_state, zero_state, norm1_g[l], norm2_g[l], mw, pw)
        new_f.append(sf)
        new_b.append(sb)
        ys, _, _ = _layer(ys, mod_lat, True, state_gla_fwd[:, l], state_gla_bwd[:, l],
                          norm1_g[l], norm2_g[l], mw, pw)
    y_prompt = _rmsnorm(yp, final_norm_g)
    y_sample = _rmsnorm(ys, final_norm_g)
    new_state_gla_fwd = jnp.stack(new_f, axis=1)
    new_state_gla_bwd = jnp.stack(new_b, axis=1)
    return (y_prompt, y_sample, new_state_gla_fwd, new_state_gla_bwd)
```

```python
import functools

import jax
import jax.numpy as jnp
from jax import lax
from jax.experimental import pallas as pl
from jax.experimental.pallas import tpu as pltpu

F32 = jnp.float32
BF16 = jnp.bfloat16

D_MODEL = 1024
N_CTX_SEQ = 32
CTX_LEN = 256
N_LAT_SEQ = 8
LAT_LEN = 1024
N_CTX_TOK = N_CTX_SEQ * CTX_LEN
N_LAT_TOK = N_LAT_SEQ * LAT_LEN
N_TOK = N_CTX_TOK + N_LAT_TOK
GRID_W = 64
GRID_H = LAT_LEN // GRID_W
GLA_HEADS = 4
GLA_DK = 256
GLA_DV = 512
GLA_HK = 64
GLA_HV = 128
GLA_RANK = 16
GLA_NORMALIZER = 16.0
CHUNK = 64
CONV_CH = 512
CONV_K = 31
CONV_PAD = CONV_K // 2
PEER_HEADS = 8
PEER_NKEYS = 128
PEER_N = PEER_NKEYS * PEER_NKEYS
PEER_DQ = 256
PEER_TOPK = 16
EPS = 1e-6
N_MOD_ROWS = 1 + N_LAT_SEQ
N_MOD_PAD = 16

LANES = 128
SUBLANES = 8
VMEM_LIMIT_BYTES = 56 * 1024 * 1024

TOK_TILE = 256
SEL_TILE = LANES
EXP_TOK_TILE = 512
EXP_TILE = 256
EXP_ROWS_PER_TILE = EXP_TILE // PEER_NKEYS
SEL_ROWS_PER_HEAD = 4 * PEER_NKEYS


def _params(*semantics):
    return pltpu.CompilerParams(dimension_semantics=semantics, vmem_limit_bytes=VMEM_LIMIT_BYTES)


def _mod_row(tile, tile_rows):
    first = tile * tile_rows
    return jnp.where(first < N_CTX_TOK, 0, 1 + (first - N_CTX_TOK) // LAT_LEN)


def _split_bf16(a):
    hi = a.astype(BF16)
    lo = (a - hi.astype(F32)).astype(BF16)
    return hi, lo


def _dot(a, b):
    return jnp.dot(a, b, preferred_element_type=F32)


def _dot_nt(a, b):
    return lax.dot_general(a, b, (((1,), (1,)), ((), ())), preferred_element_type=F32)


def _dot_tn(a, b):
    return lax.dot_general(a, b, (((0,), (0,)), ((), ())), preferred_element_type=F32)


def _mod_kernel(c_ref, w_ref, b_ref, o_ref):
    c = c_ref[...]
    a = c * jax.nn.sigmoid(c)
    a_hi, a_lo = _split_bf16(a)
    w_hi, w_lo = _split_bf16(w_ref[...])
    o_ref[...] = _dot(a_hi, w_hi) + _dot(a_hi, w_lo) + _dot(a_lo, w_hi) + b_ref[...]


def _modulation(c_all, w_mod, b_mod):
    n_out = w_mod.shape[1]
    tile = 1024
    return pl.pallas_call(
        _mod_kernel,
        grid=(n_out // tile,),
        in_specs=[
            pl.BlockSpec((N_MOD_PAD, D_MODEL), lambda j: (0, 0)),
            pl.BlockSpec((D_MODEL, tile), lambda j: (0, j)),
            pl.BlockSpec((1, tile), lambda j: (0, j)),
        ],
        out_specs=pl.BlockSpec((N_MOD_PAD, tile), lambda j: (0, j)),
        out_shape=jax.ShapeDtypeStruct((N_MOD_PAD, n_out), F32),
        compiler_params=_params("parallel"),
        name="mod",
    )(c_all, w_mod, b_mod)


N_MAIN = 2 * GLA_DK + 2 * GLA_DV + 2 * CONV_CH


def _mixin_kernel(x_ref, mod_ref, g1_ref, wm_ref, wab_ref, wup_ref, bup_ref,
                  q_ref, k_ref, v_ref, gf_ref, gb_ref, sg_ref, u_ref):
    x = x_ref[...]
    mod = mod_ref[0]
    shift1 = mod[:, 0:D_MODEL]
    scale1 = mod[:, D_MODEL:2 * D_MODEL]
    ms = jnp.mean(x * x, axis=-1, keepdims=True)
    h = (x * lax.rsqrt(ms + EPS)) * g1_ref[...]
    h = h * (1.0 + scale1) + shift1
    hb = h.astype(BF16)
    p = _dot(hb, wm_ref[...])
    q_ref[...] = p[:, 0:256] * (GLA_HK ** -0.5)
    k_ref[...] = p[:, 256:512]
    v_ref[...] = p[:, 512:1024]
    g = p[:, 1024:1536]
    sg_ref[...] = g * jax.nn.sigmoid(g)
    ca = p[:, 1536:2048]
    cb = p[:, 2048:2560]
    u_ref[...] = ca * jax.nn.sigmoid(cb)
    a = _dot(hb, wab_ref[...])
    z = _dot(a.astype(BF16), wup_ref[...]) + bup_ref[...]
    ls = jnp.minimum(z, 0.0) - jnp.log(1.0 + jnp.exp(-jnp.abs(z)))
    gl = ls * (1.0 / GLA_NORMALIZER)
    gf_ref[...] = gl[:, 0:GLA_DK]
    gb_ref[...] = gl[:, GLA_DK:2 * GLA_DK]


def _mixin(x, mod3, g1, w_main, w_ab, w_up, b_up):
    tm = TOK_TILE
    row = lambda i: (i, 0)
    const = lambda i: (0, 0)
    outs = [(GLA_DK, F32), (GLA_DK, F32), (GLA_DV, F32), (GLA_DK, F32), (GLA_DK, F32),
            (GLA_DV, F32), (CONV_CH, F32)]
    return pl.pallas_call(
        _mixin_kernel,
        grid=(N_TOK // tm,),
        in_specs=[
            pl.BlockSpec((tm, D_MODEL), row),
            pl.BlockSpec((1, 1, 6 * D_MODEL), lambda i: (_mod_row(i, tm), 0, 0)),
            pl.BlockSpec((1, D_MODEL), const),
            pl.BlockSpec((D_MODEL, N_MAIN), const),
            pl.BlockSpec((D_MODEL, LANES), const),
            pl.BlockSpec((LANES, 2 * GLA_DK), const),
            pl.BlockSpec((1, 2 * GLA_DK), const),
        ],
        out_specs=[pl.BlockSpec((tm, n), row) for n, _ in outs],
        out_shape=[jax.ShapeDtypeStruct((N_TOK, n), dt) for n, dt in outs],
        compiler_params=_params("parallel"),
        name="mixin",
    )(x, mod3, g1, w_main, w_ab, w_up, b_up)


def _gla_kernel(*refs, seq_len, has_init, emit_state):
    q_ref, k_ref, v_ref, gf_ref, gb_ref, sg_ref, gn_ref = refs[:7]
    pos = 7
    if has_init:
        s0f_ref, s0b_ref = refs[pos:pos + 2]
        pos += 2
    pos += 1
    o_ref = refs[pos]
    pos += 1
    if emit_state:
        sf_ref, sb_ref = refs[pos:pos + 2]
        pos += 2
    oacc_ref, st_ref = refs[pos:pos + 2]

    n_chunks = seq_len // CHUNK
    rows_i = lax.broadcasted_iota(jnp.int32, (CHUNK, CHUNK), 0)
    cols_i = lax.broadcasted_iota(jnp.int32, (CHUNK, CHUNK), 1)
    lane = lax.broadcasted_iota(jnp.int32, (1, GLA_DK), 1)
    head_mask = [(lane >= h * GLA_HK) & (lane < (h + 1) * GLA_HK) for h in range(GLA_HEADS)]
    srow = lax.broadcasted_iota(jnp.int32, (GLA_HEADS * CHUNK, CHUNK), 0) % CHUNK
    scol = lax.broadcasted_iota(jnp.int32, (GLA_HEADS * CHUNK, CHUNK), 1)
    ones_t = jnp.ones((CHUNK, GLA_HV), BF16)

    def run(g_ref, forward, accumulate):
        if forward:
            tri = (rows_i >= cols_i).astype(BF16)
            causal = srow >= scol
            ref_row, last_row = CHUNK // 2 - 1, CHUNK - 1
        else:
            tri = (rows_i <= cols_i).astype(BF16)
            causal = srow <= scol
            ref_row, last_row = CHUNK // 2, 0

        def step(it, carry):
            c = it if forward else n_chunks - 1 - it
            r0 = pl.multiple_of(c * CHUNK, CHUNK)
            rows = pl.ds(r0, CHUNK)
            g = g_ref[rows, :]
            q = q_ref[rows, :]
            k = k_ref[rows, :]
            vb = v_ref[rows, :].astype(BF16)
            g_hi, g_lo = _split_bf16(g)
            b = _dot(tri, g_hi) + _dot(tri, g_lo)
            b_mid = b[ref_row:ref_row + 1, :]
            b_end = b[last_row:last_row + 1, :]
            qt = q * jnp.exp(b - b_mid)
            kt = (k * jnp.exp(b_mid - b)).astype(BF16)
            qe = q * jnp.exp(b)
            kd = (k * jnp.exp(b_end - b)).astype(BF16)
            q_stack = jnp.concatenate([jnp.where(m, qt, 0.0) for m in head_mask], axis=0).astype(BF16)
            qe_stack = jnp.concatenate([jnp.where(m, qe, 0.0) for m in head_mask], axis=0).astype(BF16)
            scores = jnp.where(causal, _dot_nt(q_stack, kt), 0.0)
            o_intra = _dot(scores.astype(BF16), vb)
            state = st_ref[...]
            o_inter = _dot(qe_stack, state.astype(BF16))
            upd = _dot_tn(kd, vb)
            dec = jnp.exp(_dot_tn(g_hi, ones_t) + _dot_tn(g_lo, ones_t))
            for h in range(GLA_HEADS):
                hr = slice(h * CHUNK, (h + 1) * CHUNK)
                hv = slice(h * GLA_HV, (h + 1) * GLA_HV)
                o_h = o_intra[hr, hv] + o_inter[hr, :]
                if accumulate:
                    oacc_ref[rows, hv] += o_h
                else:
                    oacc_ref[rows, hv] = o_h
                st_ref[hr, :] = dec[hr, :] * state[hr, :] + upd[hr, hv]
            return carry

        lax.fori_loop(0, n_chunks, step, 0)

    for forward, g_ref in ((True, gf_ref), (False, gb_ref)):
        if has_init:
            st_ref[...] = (s0f_ref if forward else s0b_ref)[0].reshape(GLA_HEADS * GLA_HK, GLA_HV)
        else:
            st_ref[...] = jnp.zeros_like(st_ref)
        run(g_ref, forward, accumulate=not forward)
        if emit_state:
            (sf_ref if forward else sb_ref)[0] = st_ref[...].reshape(GLA_HEADS, GLA_HK, GLA_HV)

    o = oacc_ref[...]
    gn = gn_ref[...]
    sg = sg_ref[...]
    for h in range(GLA_HEADS):
        hv = slice(h * GLA_HV, (h + 1) * GLA_HV)
        oh = o[:, hv]
        ms = jnp.mean(oh * oh, axis=-1, keepdims=True)
        o_ref[:, hv] = ((oh * lax.rsqrt(ms + EPS)) * gn[:, hv] * sg[:, hv]).astype(o_ref.dtype)


def _gla(q, k, v, gf, gb, sg, gn, prev_out, *, seq_len, n_seq, first_block, init_states, emit_state):
    row = lambda i: (first_block + i, 0)
    const = lambda i: (0, 0)
    st_spec = pl.BlockSpec((1, GLA_HEADS, GLA_HK, GLA_HV), lambda i: (i, 0, 0, 0))
    in_specs = [
        pl.BlockSpec((seq_len, GLA_DK), row), pl.BlockSpec((seq_len, GLA_DK), row),
        pl.BlockSpec((seq_len, GLA_DV), row), pl.BlockSpec((seq_len, GLA_DK), row),
        pl.BlockSpec((seq_len, GLA_DK), row), pl.BlockSpec((seq_len, GLA_DV), row),
        pl.BlockSpec((1, GLA_DV), const),
    ]
    args = [q, k, v, gf, gb, sg, gn]
    if init_states is not None:
        in_specs += [st_spec, st_spec]
        args += list(init_states)
    in_specs.append(pl.BlockSpec(memory_space=pl.ANY))
    args.append(prev_out)
    out_specs = [pl.BlockSpec((seq_len, GLA_DV), row)]
    out_shape = [jax.ShapeDtypeStruct(prev_out.shape, prev_out.dtype)]
    if emit_state:
        out_specs += [st_spec, st_spec]
        out_shape += [jax.ShapeDtypeStruct((n_seq, GLA_HEADS, GLA_HK, GLA_HV), F32)] * 2
    return pl.pallas_call(
        functools.partial(_gla_kernel, seq_len=seq_len, has_init=init_states is not None, emit_state=emit_state),
        grid=(n_seq,),
        in_specs=in_specs,
        out_specs=out_specs,
        out_shape=out_shape,
        scratch_shapes=[pltpu.VMEM((seq_len, GLA_DV), F32), pltpu.VMEM((GLA_HEADS * GLA_HK, GLA_HV), F32)],
        input_output_aliases={len(args) - 1: 0},
        compiler_params=_params("parallel"),
        name="gla_lat" if init_states is not None else "gla_ctx",
    )(*args)


ROW_PITCH = GRID_W + 2 * (CONV_PAD + 1)


def _conv_kernel(u_ref, w_ref, b_ref, lng_ref, lnb_ref, pw_ref, prev_ref, o_ref, pad_ref, y_ref, *, latent, seq_len):
    del prev_ref
    half = CONV_CH // 2
    blk = GRID_W
    n_blk = seq_len // blk
    pad_ref[...] = jnp.zeros_like(pad_ref)
    if latent:
        for r in range(n_blk):
            pad_ref[r * ROW_PITCH + CONV_PAD + 1:r * ROW_PITCH + CONV_PAD + 1 + blk, :] = u_ref[r * blk:(r + 1) * blk, 0:half]
    else:
        pad_ref[CONV_PAD + 1:CONV_PAD + 1 + seq_len, :] = u_ref[...]

    bias = b_ref[...]
    lng = lng_ref[...]
    lnb = lnb_ref[...]
    for r in range(n_blk):
        if latent:
            acc_w = jnp.zeros((blk, half), F32)
            for t in range(CONV_K):
                acc_w += w_ref[t:t + 1, 0:half] * pad_ref[r * ROW_PITCH + t + 1:r * ROW_PITCH + t + 1 + blk, :]
            acc_h = jnp.zeros((blk, half), F32)
            for r2 in range(n_blk):
                t = r2 - r + CONV_PAD
                acc_h += w_ref[t:t + 1, half:CONV_CH] * u_ref[r2 * blk:(r2 + 1) * blk, half:CONV_CH]
            y = jnp.concatenate([acc_w, acc_h], axis=-1) + bias
        else:
            acc = jnp.zeros((blk, CONV_CH), F32)
            for t in range(CONV_K):
                acc += w_ref[t:t + 1, :] * pad_ref[r * blk + t + 1:r * blk + t + 1 + blk, :]
            y = acc + bias
        mu = jnp.mean(y, axis=-1, keepdims=True)
        yc = y - mu
        var = jnp.mean(yc * yc, axis=-1, keepdims=True)
        yn = (yc * lax.rsqrt(var + EPS)) * lng + lnb
        y_ref[r * blk:(r + 1) * blk, :] = (yn * jax.nn.sigmoid(yn)).astype(BF16)
    o_ref[...] = _dot(y_ref[...], pw_ref[...]).astype(o_ref.dtype)


def _conv(u, w, b, lng, lnb, pw, prev_out, *, seq_len, n_seq, first_block, latent):
    row = lambda i: (first_block + i, 0)
    const = lambda i: (0, 0)
    if latent:
        pad_shape = (GRID_H * ROW_PITCH, CONV_CH // 2)
    else:
        pad_shape = (seq_len + 2 * (CONV_PAD + 1), CONV_CH)
    return pl.pallas_call(
        functools.partial(_conv_kernel, latent=latent, seq_len=seq_len),
        grid=(n_seq,),
        in_specs=[
            pl.BlockSpec((seq_len, CONV_CH), row),
            pl.BlockSpec((CONV_K + 1, CONV_CH), const),
            pl.BlockSpec((1, CONV_CH), const), pl.BlockSpec((1, CONV_CH), const), pl.BlockSpec((1, CONV_CH), const),
            pl.BlockSpec((CONV_CH, CONV_CH), const),
            pl.BlockSpec(memory_space=pl.ANY),
        ],
        out_specs=pl.BlockSpec((seq_len, CONV_CH), row),
        out_shape=jax.ShapeDtypeStruct(prev_out.shape, prev_out.dtype),
        scratch_shapes=[pltpu.VMEM(pad_shape, F32), pltpu.VMEM((seq_len, CONV_CH), BF16)],
        input_output_aliases={6: 0},
        compiler_params=_params("parallel"),
        name="conv_lat" if latent else "conv_ctx",
    )(u, w, b, lng, lnb, pw, prev_out)


def _mid_kernel(mo_ref, uu_ref, x_ref, mod_ref, g2_ref, wo1_ref, wo2_ref, wqt_ref, keys_ref,
                x1_ref, h2t_ref, st_ref):
    mod = mod_ref[0]
    gate1 = mod[:, 2 * D_MODEL:3 * D_MODEL]
    shift2 = mod[:, 3 * D_MODEL:4 * D_MODEL]
    scale2 = mod[:, 4 * D_MODEL:5 * D_MODEL]
    m = _dot(mo_ref[...], wo1_ref[...]) + _dot(uu_ref[...], wo2_ref[...])
    x1 = x_ref[...] + gate1 * m
    x1_ref[...] = x1
    ms = jnp.mean(x1 * x1, axis=-1, keepdims=True)
    h2 = (x1 * lax.rsqrt(ms + EPS)) * g2_ref[...]
    h2 = h2 * (1.0 + scale2) + shift2
    h2t = h2.T.astype(BF16)
    h2t_ref[...] = h2t
    qt = _dot(wqt_ref[...], h2t)
    for hp in range(2 * PEER_HEADS):
        rows = slice(hp * PEER_NKEYS, (hp + 1) * PEER_NKEYS)
        st_ref[rows, :] = _dot(keys_ref[hp], qt[rows, :].astype(BF16))


def _mid(mo, uu, x, mod3, g2, wo1, wo2, wqt, keys):
    tm = TOK_TILE
    row = lambda i: (i, 0)
    col = lambda i: (0, i)
    const = lambda i: (0, 0)
    n_q = PEER_HEADS * PEER_DQ
    return pl.pallas_call(
        _mid_kernel,
        grid=(N_TOK // tm,),
        in_specs=[
            pl.BlockSpec((tm, GLA_DV), row), pl.BlockSpec((tm, CONV_CH), row), pl.BlockSpec((tm, D_MODEL), row),
            pl.BlockSpec((1, 1, 6 * D_MODEL), lambda i: (_mod_row(i, tm), 0, 0)),
            pl.BlockSpec((1, D_MODEL), const),
            pl.BlockSpec((GLA_DV, D_MODEL), const), pl.BlockSpec((CONV_CH, D_MODEL), const),
            pl.BlockSpec((n_q, D_MODEL), const),
            pl.BlockSpec((2 * PEER_HEADS, PEER_NKEYS, PEER_DQ // 2), lambda i: (0, 0, 0)),
        ],
        out_specs=[pl.BlockSpec((tm, D_MODEL), row), pl.BlockSpec((D_MODEL, tm), col), pl.BlockSpec((n_q, tm), col)],
        out_shape=[jax.ShapeDtypeStruct((N_TOK, D_MODEL), F32), jax.ShapeDtypeStruct((D_MODEL, N_TOK), BF16),
                   jax.ShapeDtypeStruct((n_q, N_TOK), F32)],
        compiler_params=_params("parallel"),
        name="mid",
    )(mo, uu, x, mod3, g2, wo1, wo2, wqt, keys)


N_CAND_ROWS = 80


def _top16(s, key_iota, sv_ref):
    rank = jnp.full(s.shape, PEER_TOPK, jnp.int32)
    for a in range(PEER_TOPK):
        m = jnp.max(s, axis=0, keepdims=True)
        first = jnp.min(jnp.where(s == m, key_iota, PEER_NKEYS), axis=0, keepdims=True)
        hit = key_iota == first
        rank = jnp.where(hit, a, rank)
        s = jnp.where(hit, -jnp.inf, s)
        sv_ref[a:a + 1, :] = m
    return rank


def _select_kernel(st_ref, pe_ref, sv1_ref, sv2_ref, cand_ref):
    key_iota = lax.broadcasted_iota(jnp.int32, (PEER_NKEYS, SEL_TILE), 0)
    crow = lax.broadcasted_iota(jnp.int32, (N_CAND_ROWS, SEL_TILE), 0)
    flat = jnp.where(crow < 16, crow * 16,
                     jnp.where(crow < 72, ((crow - 16) % 8) * 16 + (crow - 16) // 8 + 1, crow - 72 + 8))

    def head(h, carry):
        base = pl.multiple_of(h * 2 * PEER_NKEYS, 2 * PEER_NKEYS)
        s1 = st_ref[pl.ds(base, PEER_NKEYS), :]
        s2 = st_ref[pl.ds(base + PEER_NKEYS, PEER_NKEYS), :]
        r1 = _top16(s1, key_iota, sv1_ref)
        r2 = _top16(s2, key_iota, sv2_ref)
        top1 = sv1_ref[0:1, :]
        top2 = sv2_ref[0:1, :]
        cand_ref[0:16, :] = sv1_ref[...] + top2
        for b in range(1, 8):
            cand_ref[8 + 8 * b:16 + 8 * b, :] = sv1_ref[0:8, :] + sv2_ref[b:b + 1, :]
        cand_ref[72:80, :] = top1 + sv2_ref[8:16, :]
        cand = cand_ref[...]
        cmax = top1 + top2
        sel = jnp.zeros(cand.shape, jnp.bool_)
        z = jnp.zeros((1, SEL_TILE), F32)
        for _ in range(PEER_TOPK):
            m = jnp.max(cand, axis=0, keepdims=True)
            first = jnp.min(jnp.where(cand == m, flat, 256), axis=0, keepdims=True)
            hit = flat == first
            sel = sel | hit
            cand = jnp.where(hit, -jnp.inf, cand)
            z = z + jnp.exp(m - cmax)
        self32 = sel.astype(F32)
        counts = [jnp.sum(self32[0:16, :], axis=0, keepdims=True)]
        for b in range(1, 8):
            counts.append(jnp.sum(self32[8 + 8 * b:16 + 8 * b, :], axis=0, keepdims=True))
        for b in range(8, 16):
            counts.append(self32[72 + b - 8:72 + b - 7, :])
        n2 = jnp.zeros((PEER_NKEYS, SEL_TILE), F32)
        for b in range(PEER_TOPK):
            n2 = jnp.where(r2 == b, counts[b], n2)
        out = pl.multiple_of(h * SEL_ROWS_PER_HEAD, SEL_ROWS_PER_HEAD)
        pe_ref[pl.ds(out, PEER_NKEYS), :] = r1.astype(F32)
        pe_ref[pl.ds(out + PEER_NKEYS, PEER_NKEYS), :] = n2
        pe_ref[pl.ds(out + 2 * PEER_NKEYS, PEER_NKEYS), :] = jnp.exp(s1 - top1)
        pe_ref[pl.ds(out + 3 * PEER_NKEYS, PEER_NKEYS), :] = jnp.exp(s2 - top2) / z
        return carry

    lax.fori_loop(0, PEER_HEADS, head, 0)


def _select(st):
    n_q = PEER_HEADS * PEER_DQ
    n_pe = PEER_HEADS * SEL_ROWS_PER_HEAD
    col = lambda i: (0, i)
    return pl.pallas_call(
        _select_kernel,
        grid=(N_TOK // SEL_TILE,),
        in_specs=[pl.BlockSpec((n_q, SEL_TILE), col)],
        out_specs=pl.BlockSpec((n_pe, SEL_TILE), col),
        out_shape=jax.ShapeDtypeStruct((n_pe, N_TOK), F32),
        scratch_shapes=[pltpu.VMEM((PEER_TOPK, SEL_TILE), F32), pltpu.VMEM((PEER_TOPK, SEL_TILE), F32),
                        pltpu.VMEM((N_CAND_ROWS, SEL_TILE), F32)],
        compiler_params=_params("parallel"),
        name="select",
    )(st)


def _experts_kernel(h2t_ref, pe_ref, u_ref, vt_ref, x1_ref, mod_ref, gfin_ref, y_ref, acc_ref):
    e = pl.program_id(1)

    @pl.when(e == 0)
    def _():
        acc_ref[...] = jnp.zeros_like(acc_ref)

    a_t = _dot(u_ref[...], h2t_ref[...])
    act = 0.5 * a_t * (1.0 + lax.erf(a_t * (2.0 ** -0.5)))
    parts = []
    for il in range(EXP_ROWS_PER_TILE):
        i = e * EXP_ROWS_PER_TILE + il
        w = jnp.zeros((PEER_NKEYS, EXP_TOK_TILE), F32)
        for h in range(PEER_HEADS):
            base = h * SEL_ROWS_PER_HEAD
            rank_i = pe_ref[pl.ds(base + i, 1), :]
            e1_i = pe_ref[pl.ds(base + 2 * PEER_NKEYS + i, 1), :]
            n2 = pe_ref[base + PEER_NKEYS:base + 2 * PEER_NKEYS, :]
            e2 = pe_ref[base + 3 * PEER_NKEYS:base + 4 * PEER_NKEYS, :]
            w = w + jnp.where(rank_i < n2, e1_i * e2, 0.0)
        parts.append(w)
    w_all = jnp.concatenate(parts, axis=0)
    c_t = (act * w_all).astype(BF16)
    acc_ref[...] += _dot(vt_ref[...], c_t)

    @pl.when(e == pl.num_programs(1) - 1)
    def _():
        mod = mod_ref[0]
        gate2 = mod[:, 5 * D_MODEL:6 * D_MODEL]
        y = x1_ref[...] + gate2 * acc_ref[...].T
        ms = jnp.mean(y * y, axis=-1, keepdims=True)
        y_ref[...] = (y * lax.rsqrt(ms + EPS)) * gfin_ref[...]


def _experts(h2t, pe, u_bf, vt_bf, x1, mod3, gfin):
    tm = EXP_TOK_TILE
    n_pe = PEER_HEADS * SEL_ROWS_PER_HEAD
    return pl.pallas_call(
        _experts_kernel,
        grid=(N_TOK // tm, PEER_N // EXP_TILE),
        in_specs=[
            pl.BlockSpec((D_MODEL, tm), lambda i, e: (0, i)),
            pl.BlockSpec((n_pe, tm), lambda i, e: (0, i)),
            pl.BlockSpec((EXP_TILE, D_MODEL), lambda i, e: (e, 0)),
            pl.BlockSpec((D_MODEL, EXP_TILE), lambda i, e: (0, e)),
            pl.BlockSpec((tm, D_MODEL), lambda i, e: (i, 0)),
            pl.BlockSpec((1, 1, 6 * D_MODEL), lambda i, e: (_mod_row(i, tm), 0, 0)),
            pl.BlockSpec((1, D_MODEL), lambda i, e: (0, 0)),
        ],
        out_specs=pl.BlockSpec((tm, D_MODEL), lambda i, e: (i, 0)),
        out_shape=jax.ShapeDtypeStruct((N_TOK, D_MODEL), F32),
        scratch_shapes=[pltpu.VMEM((D_MODEL, tm), F32)],
        compiler_params=_params("parallel", "arbitrary"),
        name="experts",
    )(h2t, pe, u_bf, vt_bf, x1, mod3, gfin)


def kernel(x_prompt, x_sample, c, state_gla_fwd, state_gla_bwd, c_ctx, norm1_g, w_mod, b_mod, w_in, w_af_up, b_af,
           w_ab_up, b_ab, gla_norm_g, conv_dw_w, conv_dw_b, conv_ln_g, conv_ln_b, conv_pw2, w_out, norm2_g, peer_wq,
           peer_subkeys, peer_u, peer_v, final_norm_g):
    depth = w_in.shape[0]
    assert depth == 1
    l = 0
    x = jnp.concatenate([x_prompt.reshape(N_CTX_TOK, D_MODEL), x_sample.reshape(N_LAT_TOK, D_MODEL)], axis=0)

    c_all = jnp.zeros((N_MOD_PAD, D_MODEL), F32).at[0].set(c_ctx).at[1:N_MOD_ROWS].set(c)
    w = w_in[l]
    w_main = jnp.concatenate([w[:, 0:1536], w[:, 1568:2592]], axis=1).astype(BF16)
    w_ab = jnp.zeros((D_MODEL, LANES), F32).at[:, 0:2 * GLA_RANK].set(w[:, 1536:1568]).astype(BF16)
    w_up = jnp.zeros((LANES, 2 * GLA_DK), F32)
    w_up = w_up.at[0:GLA_RANK, 0:GLA_DK].set(w_af_up[l]).at[GLA_RANK:2 * GLA_RANK, GLA_DK:].set(w_ab_up[l]).astype(BF16)
    b_up = jnp.concatenate([b_af[l], b_ab[l]])[None, :]
    conv_w = jnp.zeros((CONV_K + 1, CONV_CH), F32).at[0:CONV_K].set(conv_dw_w[l])
    wo1 = w_out[l][0:GLA_DV].astype(BF16)
    wo2 = w_out[l][GLA_DV:].astype(BF16)
    wqt = peer_wq[l].T.astype(BF16)
    keys = peer_subkeys[l].reshape(2 * PEER_HEADS, PEER_NKEYS, PEER_DQ // 2).astype(BF16)
    u_bf = peer_u[l].astype(BF16)
    vt_bf = peer_v[l].T.astype(BF16)

    mod = _modulation(c_all, w_mod[l], b_mod[l][None, :])
    mod3 = mod[:, None, :]

    q, k, v, gf, gb, sg, u = _mixin(x, mod3, norm1_g[l][None, :], w_main, w_ab, w_up, b_up)

    gn = gla_norm_g[l][None, :]
    mo0 = jnp.zeros((N_TOK, GLA_DV), BF16)
    mo1, new_f, new_b = _gla(q, k, v, gf, gb, sg, gn, mo0, seq_len=CTX_LEN, n_seq=N_CTX_SEQ, first_block=0,
                             init_states=None, emit_state=True)
    (mo,) = _gla(q, k, v, gf, gb, sg, gn, mo1, seq_len=LAT_LEN, n_seq=N_LAT_SEQ,
                 first_block=N_CTX_TOK // LAT_LEN, init_states=(state_gla_fwd[:, l], state_gla_bwd[:, l]),
                 emit_state=False)

    conv_args = (conv_w, conv_dw_b[l][None, :], conv_ln_g[l][None, :], conv_ln_b[l][None, :], conv_pw2[l].astype(BF16))
    uu0 = jnp.zeros((N_TOK, CONV_CH), BF16)
    uu1 = _conv(u, *conv_args, uu0, seq_len=CTX_LEN, n_seq=N_CTX_SEQ, first_block=0, latent=False)
    uu = _conv(u, *conv_args, uu1, seq_len=LAT_LEN, n_seq=N_LAT_SEQ, first_block=N_CTX_TOK // LAT_LEN, latent=True)

    x1, h2t, st = _mid(mo, uu, x, mod3, norm2_g[l][None, :], wo1, wo2, wqt, keys)
    pe = _select(st)
    y = _experts(h2t, pe, u_bf, vt_bf, x1, mod3, final_norm_g[None, :])

    y_prompt = y[0:N_CTX_TOK].reshape(N_CTX_SEQ, CTX_LEN, D_MODEL)
    y_sample = y[N_CTX_TOK:].reshape(N_LAT_SEQ, LAT_LEN, D_MODEL)
    return (y_prompt, y_sample, new_f[:, None], new_b[:, None])
```

```python
import functools

import jax
import jax.numpy as jnp
from jax import lax
from jax.experimental import pallas as pl
from jax.experimental.pallas import tpu as pltpu

F32 = jnp.float32
BF16 = jnp.bfloat16

D_MODEL = 1024
N_CTX_SEQ = 32
CTX_LEN = 256
N_LAT_SEQ = 8
LAT_LEN = 1024
N_CTX_TOK = N_CTX_SEQ * CTX_LEN
N_LAT_TOK = N_LAT_SEQ * LAT_LEN
N_TOK = N_CTX_TOK + N_LAT_TOK
GRID_W = 64
GRID_H = LAT_LEN // GRID_W
GLA_HEADS = 4
GLA_DK = 256
GLA_DV = 512
GLA_HK = 64
GLA_HV = 128
GLA_RANK = 16
GLA_NORMALIZER = 16.0
CHUNK = 64
CONV_CH = 512
CONV_K = 31
CONV_PAD = CONV_K // 2
PEER_HEADS = 8
PEER_NKEYS = 128
PEER_N = PEER_NKEYS * PEER_NKEYS
PEER_DQ = 256
PEER_TOPK = 16
EPS = 1e-6
N_MOD_ROWS = 1 + N_LAT_SEQ
N_MOD_PAD = 16

LANES = 128
SUBLANES = 8
VMEM_LIMIT_BYTES = 56 * 1024 * 1024

TOK_TILE = 256
SEL_TILE = LANES
EXP_TOK_TILE = 512
EXP_TILE = 1024
EXP_SUB = 256
SEL_ROWS_PER_HEAD = 2 * PEER_NKEYS
BF16_ROWS = 16


def _params(*semantics):
    return pltpu.CompilerParams(dimension_semantics=semantics, vmem_limit_bytes=VMEM_LIMIT_BYTES)


def _mod_row(tile, tile_rows):
    first = tile * tile_rows
    return jnp.where(first < N_CTX_TOK, 0, 1 + (first - N_CTX_TOK) // LAT_LEN)


def _split_bf16(a):
    hi = a.astype(BF16)
    lo = (a - hi.astype(F32)).astype(BF16)
    return hi, lo


def _dot(a, b):
    return jnp.dot(a, b, preferred_element_type=F32)


def _dot_nt(a, b):
    return lax.dot_general(a, b, (((1,), (1,)), ((), ())), preferred_element_type=F32)


def _dot_tn(a, b):
    return lax.dot_general(a, b, (((0,), (0,)), ((), ())), preferred_element_type=F32)


def _mod_kernel(c_ref, w_ref, b_ref, o_ref):
    c = c_ref[...]
    a = c * jax.nn.sigmoid(c)
    a_hi, a_lo = _split_bf16(a)
    w_hi, w_lo = _split_bf16(w_ref[...])
    o_ref[...] = _dot(a_hi, w_hi) + _dot(a_hi, w_lo) + _dot(a_lo, w_hi) + b_ref[...]


def _modulation(c_all, w_mod, b_mod):
    n_out = w_mod.shape[1]
    tile = 1024
    return pl.pallas_call(
        _mod_kernel,
        grid=(n_out // tile,),
        in_specs=[
            pl.BlockSpec((N_MOD_PAD, D_MODEL), lambda j: (0, 0)),
            pl.BlockSpec((D_MODEL, tile), lambda j: (0, j)),
            pl.BlockSpec((1, tile), lambda j: (0, j)),
        ],
        out_specs=pl.BlockSpec((N_MOD_PAD, tile), lambda j: (0, j)),
        out_shape=jax.ShapeDtypeStruct((N_MOD_PAD, n_out), F32),
        compiler_params=_params("parallel"),
        name="mod",
    )(c_all, w_mod, b_mod)


N_MAIN = 2 * GLA_DK + 2 * GLA_DV + 2 * CONV_CH


def _mixin_kernel(x_ref, mod_ref, g1_ref, wm_ref, wab_ref, wup_ref, bup_ref,
                  q_ref, k_ref, v_ref, gf_ref, gb_ref, sg_ref, u_ref):
    x = x_ref[...]
    mod = mod_ref[0]
    shift1 = mod[:, 0:D_MODEL]
    scale1 = mod[:, D_MODEL:2 * D_MODEL]
    ms = jnp.mean(x * x, axis=-1, keepdims=True)
    h = (x * lax.rsqrt(ms + EPS)) * g1_ref[...]
    h = h * (1.0 + scale1) + shift1
    hb = h.astype(BF16)
    p = _dot(hb, wm_ref[...])
    q_ref[...] = p[:, 0:256] * (GLA_HK ** -0.5)
    k_ref[...] = p[:, 256:512]
    v_ref[...] = p[:, 512:1024]
    g = p[:, 1024:1536]
    sg_ref[...] = g * jax.nn.sigmoid(g)
    ca = p[:, 1536:2048]
    cb = p[:, 2048:2560]
    u_ref[...] = ca * jax.nn.sigmoid(cb)
    a = _dot(hb, wab_ref[...])
    z = _dot(a.astype(BF16), wup_ref[...]) + bup_ref[...]
    ls = jnp.minimum(z, 0.0) - jnp.log(1.0 + jnp.exp(-jnp.abs(z)))
    gl = ls * (1.0 / GLA_NORMALIZER)
    gf_ref[...] = gl[:, 0:GLA_DK]
    gb_ref[...] = gl[:, GLA_DK:2 * GLA_DK]


def _mixin(x, mod3, g1, w_main, w_ab, w_up, b_up):
    tm = TOK_TILE
    row = lambda i: (i, 0)
    const = lambda i: (0, 0)
    outs = [(GLA_DK, F32), (GLA_DK, F32), (GLA_DV, F32), (GLA_DK, F32), (GLA_DK, F32),
            (GLA_DV, F32), (CONV_CH, F32)]
    return pl.pallas_call(
        _mixin_kernel,
        grid=(N_TOK // tm,),
        in_specs=[
            pl.BlockSpec((tm, D_MODEL), row),
            pl.BlockSpec((1, 1, 6 * D_MODEL), lambda i: (_mod_row(i, tm), 0, 0)),
            pl.BlockSpec((1, D_MODEL), const),
            pl.BlockSpec((D_MODEL, N_MAIN), const),
            pl.BlockSpec((D_MODEL, LANES), const),
            pl.BlockSpec((LANES, 2 * GLA_DK), const),
            pl.BlockSpec((1, 2 * GLA_DK), const),
        ],
        out_specs=[pl.BlockSpec((tm, n), row) for n, _ in outs],
        out_shape=[jax.ShapeDtypeStruct((N_TOK, n), dt) for n, dt in outs],
        compiler_params=_params("parallel"),
        name="mixin",
    )(x, mod3, g1, w_main, w_ab, w_up, b_up)


def _gla_kernel(*refs, seq_len, has_init, emit_state):
    q_ref, k_ref, v_ref, gf_ref, gb_ref, sg_ref, gn_ref = refs[:7]
    pos = 7
    if has_init:
        s0f_ref, s0b_ref = refs[pos:pos + 2]
        pos += 2
    pos += 1
    o_ref = refs[pos]
    pos += 1
    if emit_state:
        sf_ref, sb_ref = refs[pos:pos + 2]
        pos += 2
    oacc_ref, st_ref = refs[pos:pos + 2]

    n_chunks = seq_len // CHUNK
    rows_i = lax.broadcasted_iota(jnp.int32, (CHUNK, CHUNK), 0)
    cols_i = lax.broadcasted_iota(jnp.int32, (CHUNK, CHUNK), 1)
    lane = lax.broadcasted_iota(jnp.int32, (1, GLA_DK), 1)
    head_mask = [(lane >= h * GLA_HK) & (lane < (h + 1) * GLA_HK) for h in range(GLA_HEADS)]
    srow = lax.broadcasted_iota(jnp.int32, (GLA_HEADS * CHUNK, CHUNK), 0) % CHUNK
    scol = lax.broadcasted_iota(jnp.int32, (GLA_HEADS * CHUNK, CHUNK), 1)
    ones_t = jnp.ones((CHUNK, GLA_HV), BF16)

    def run(g_ref, forward, accumulate):
        if forward:
            tri = (rows_i >= cols_i).astype(BF16)
            causal = srow >= scol
            ref_row, last_row = CHUNK // 2 - 1, CHUNK - 1
        else:
            tri = (rows_i <= cols_i).astype(BF16)
            causal = srow <= scol
            ref_row, last_row = CHUNK // 2, 0

        def step(it, carry):
            c = it if forward else n_chunks - 1 - it
            r0 = pl.multiple_of(c * CHUNK, CHUNK)
            rows = pl.ds(r0, CHUNK)
            g = g_ref[rows, :]
            q = q_ref[rows, :]
            k = k_ref[rows, :]
            vb = v_ref[rows, :].astype(BF16)
            g_hi, g_lo = _split_bf16(g)
            b = _dot(tri, g_hi) + _dot(tri, g_lo)
            b_mid = b[ref_row:ref_row + 1, :]
            b_end = b[last_row:last_row + 1, :]
            qt = q * jnp.exp(b - b_mid)
            kt = (k * jnp.exp(b_mid - b)).astype(BF16)
            qe = q * jnp.exp(b)
            kd = (k * jnp.exp(b_end - b)).astype(BF16)
            q_stack = jnp.concatenate([jnp.where(m, qt, 0.0) for m in head_mask], axis=0).astype(BF16)
            qe_stack = jnp.concatenate([jnp.where(m, qe, 0.0) for m in head_mask], axis=0).astype(BF16)
            scores = jnp.where(causal, _dot_nt(q_stack, kt), 0.0)
            o_intra = _dot(scores.astype(BF16), vb)
            state = st_ref[...]
            o_inter = _dot(qe_stack, state.astype(BF16))
            upd = _dot_tn(kd, vb)
            dec = jnp.exp(_dot_tn(g_hi, ones_t) + _dot_tn(g_lo, ones_t))
            for h in range(GLA_HEADS):
                hr = slice(h * CHUNK, (h + 1) * CHUNK)
                hv = slice(h * GLA_HV, (h + 1) * GLA_HV)
                o_h = o_intra[hr, hv] + o_inter[hr, :]
                if accumulate:
                    oacc_ref[rows, hv] += o_h
                else:
                    oacc_ref[rows, hv] = o_h
                st_ref[hr, :] = dec[hr, :] * state[hr, :] + upd[hr, hv]
            return carry

        lax.fori_loop(0, n_chunks, step, 0)

    for forward, g_ref in ((True, gf_ref), (False, gb_ref)):
        if has_init:
            st_ref[...] = (s0f_ref if forward else s0b_ref)[0].reshape(GLA_HEADS * GLA_HK, GLA_HV)
        else:
            st_ref[...] = jnp.zeros_like(st_ref)
        run(g_ref, forward, accumulate=not forward)
        if emit_state:
            (sf_ref if forward else sb_ref)[0] = st_ref[...].reshape(GLA_HEADS, GLA_HK, GLA_HV)

    o = oacc_ref[...]
    gn = gn_ref[...]
    sg = sg_ref[...]
    for h in range(GLA_HEADS):
        hv = slice(h * GLA_HV, (h + 1) * GLA_HV)
        oh = o[:, hv]
        ms = jnp.mean(oh * oh, axis=-1, keepdims=True)
        o_ref[:, hv] = ((oh * lax.rsqrt(ms + EPS)) * gn[:, hv] * sg[:, hv]).astype(o_ref.dtype)


def _gla(q, k, v, gf, gb, sg, gn, prev_out, *, seq_len, n_seq, first_block, init_states, emit_state):
    row = lambda i: (first_block + i, 0)
    const = lambda i: (0, 0)
    st_spec = pl.BlockSpec((1, GLA_HEADS, GLA_HK, GLA_HV), lambda i: (i, 0, 0, 0))
    in_specs = [
        pl.BlockSpec((seq_len, GLA_DK), row), pl.BlockSpec((seq_len, GLA_DK), row),
        pl.BlockSpec((seq_len, GLA_DV), row), pl.BlockSpec((seq_len, GLA_DK), row),
        pl.BlockSpec((seq_len, GLA_DK), row), pl.BlockSpec((seq_len, GLA_DV), row),
        pl.BlockSpec((1, GLA_DV), const),
    ]
    args = [q, k, v, gf, gb, sg, gn]
    if init_states is not None:
        in_specs += [st_spec, st_spec]
        args += list(init_states)
    in_specs.append(pl.BlockSpec(memory_space=pl.ANY))
    args.append(prev_out)
    out_specs = [pl.BlockSpec((seq_len, GLA_DV), row)]
    out_shape = [jax.ShapeDtypeStruct(prev_out.shape, prev_out.dtype)]
    if emit_state:
        out_specs += [st_spec, st_spec]
        out_shape += [jax.ShapeDtypeStruct((n_seq, GLA_HEADS, GLA_HK, GLA_HV), F32)] * 2
    return pl.pallas_call(
        functools.partial(_gla_kernel, seq_len=seq_len, has_init=init_states is not None, emit_state=emit_state),
        grid=(n_seq,),
        in_specs=in_specs,
        out_specs=out_specs,
        out_shape=out_shape,
        scratch_shapes=[pltpu.VMEM((seq_len, GLA_DV), F32), pltpu.VMEM((GLA_HEADS * GLA_HK, GLA_HV), F32)],
        input_output_aliases={len(args) - 1: 0},
        compiler_params=_params("parallel"),
        name="gla_lat" if init_states is not None else "gla_ctx",
    )(*args)


ROW_PITCH = GRID_W + 2 * (CONV_PAD + 1)


def _conv_kernel(u_ref, w_ref, b_ref, lng_ref, lnb_ref, pw_ref, prev_ref, o_ref, pad_ref, y_ref, *, latent, seq_len):
    del prev_ref
    half = CONV_CH // 2
    blk = GRID_W
    n_blk = seq_len // blk
    pad_ref[...] = jnp.zeros_like(pad_ref)
    if latent:
        for r in range(n_blk):
            pad_ref[r * ROW_PITCH + CONV_PAD + 1:r * ROW_PITCH + CONV_PAD + 1 + blk, :] = u_ref[r * blk:(r + 1) * blk, 0:half]
    else:
        pad_ref[CONV_PAD + 1:CONV_PAD + 1 + seq_len, :] = u_ref[...]

    bias = b_ref[...]
    lng = lng_ref[...]
    lnb = lnb_ref[...]
    for r in range(n_blk):
        if latent:
            acc_w = jnp.zeros((blk, half), F32)
            for t in range(CONV_K):
                acc_w += w_ref[t:t + 1, 0:half] * pad_ref[r * ROW_PITCH + t + 1:r * ROW_PITCH + t + 1 + blk, :]
            acc_h = jnp.zeros((blk, half), F32)
            for r2 in range(n_blk):
                t = r2 - r + CONV_PAD
                acc_h += w_ref[t:t + 1, half:CONV_CH] * u_ref[r2 * blk:(r2 + 1) * blk, half:CONV_CH]
            y = jnp.concatenate([acc_w, acc_h], axis=-1) + bias
        else:
            acc = jnp.zeros((blk, CONV_CH), F32)
            for t in range(CONV_K):
                acc += w_ref[t:t + 1, :] * pad_ref[r * blk + t + 1:r * blk + t + 1 + blk, :]
            y = acc + bias
        mu = jnp.mean(y, axis=-1, keepdims=True)
        yc = y - mu
        var = jnp.mean(yc * yc, axis=-1, keepdims=True)
        yn = (yc * lax.rsqrt(var + EPS)) * lng + lnb
        y_ref[r * blk:(r + 1) * blk, :] = (yn * jax.nn.sigmoid(yn)).astype(BF16)
    o_ref[...] = _dot(y_ref[...], pw_ref[...]).astype(o_ref.dtype)


def _conv(u, w, b, lng, lnb, pw, prev_out, *, seq_len, n_seq, first_block, latent):
    row = lambda i: (first_block + i, 0)
    const = lambda i: (0, 0)
    if latent:
        pad_shape = (GRID_H * ROW_PITCH, CONV_CH // 2)
    else:
        pad_shape = (seq_len + 2 * (CONV_PAD + 1), CONV_CH)
    return pl.pallas_call(
        functools.partial(_conv_kernel, latent=latent, seq_len=seq_len),
        grid=(n_seq,),
        in_specs=[
            pl.BlockSpec((seq_len, CONV_CH), row),
            pl.BlockSpec((CONV_K + 1, CONV_CH), const),
            pl.BlockSpec((1, CONV_CH), const), pl.BlockSpec((1, CONV_CH), const), pl.BlockSpec((1, CONV_CH), const),
            pl.BlockSpec((CONV_CH, CONV_CH), const),
            pl.BlockSpec(memory_space=pl.ANY),
        ],
        out_specs=pl.BlockSpec((seq_len, CONV_CH), row),
        out_shape=jax.ShapeDtypeStruct(prev_out.shape, prev_out.dtype),
        scratch_shapes=[pltpu.VMEM(pad_shape, F32), pltpu.VMEM((seq_len, CONV_CH), BF16)],
        input_output_aliases={6: 0},
        compiler_params=_params("parallel"),
        name="conv_lat" if latent else "conv_ctx",
    )(u, w, b, lng, lnb, pw, prev_out)


def _mid_kernel(mo_ref, uu_ref, x_ref, mod_ref, g2_ref, wo1_ref, wo2_ref, wqt_ref, keys_ref,
                x1_ref, h2t_ref, st_ref):
    mod = mod_ref[0]
    gate1 = mod[:, 2 * D_MODEL:3 * D_MODEL]
    shift2 = mod[:, 3 * D_MODEL:4 * D_MODEL]
    scale2 = mod[:, 4 * D_MODEL:5 * D_MODEL]
    m = _dot(mo_ref[...], wo1_ref[...]) + _dot(uu_ref[...], wo2_ref[...])
    x1 = x_ref[...] + gate1 * m
    x1_ref[...] = x1
    ms = jnp.mean(x1 * x1, axis=-1, keepdims=True)
    h2 = (x1 * lax.rsqrt(ms + EPS)) * g2_ref[...]
    h2 = h2 * (1.0 + scale2) + shift2
    h2t = h2.T.astype(BF16)
    h2t_ref[...] = h2t
    qt = _dot(wqt_ref[...], h2t)
    for hp in range(2 * PEER_HEADS):
        rows = slice(hp * PEER_NKEYS, (hp + 1) * PEER_NKEYS)
        st_ref[rows, :] = _dot(keys_ref[hp], qt[rows, :].astype(BF16))


def _mid(mo, uu, x, mod3, g2, wo1, wo2, wqt, keys):
    tm = TOK_TILE
    row = lambda i: (i, 0)
    col = lambda i: (0, i)
    const = lambda i: (0, 0)
    n_q = PEER_HEADS * PEER_DQ
    return pl.pallas_call(
        _mid_kernel,
        grid=(N_TOK // tm,),
        in_specs=[
            pl.BlockSpec((tm, GLA_DV), row), pl.BlockSpec((tm, CONV_CH), row), pl.BlockSpec((tm, D_MODEL), row),
            pl.BlockSpec((1, 1, 6 * D_MODEL), lambda i: (_mod_row(i, tm), 0, 0)),
            pl.BlockSpec((1, D_MODEL), const),
            pl.BlockSpec((GLA_DV, D_MODEL), const), pl.BlockSpec((CONV_CH, D_MODEL), const),
            pl.BlockSpec((n_q, D_MODEL), const),
            pl.BlockSpec((2 * PEER_HEADS, PEER_NKEYS, PEER_DQ // 2), lambda i: (0, 0, 0)),
        ],
        out_specs=[pl.BlockSpec((tm, D_MODEL), row), pl.BlockSpec((D_MODEL, tm), col), pl.BlockSpec((n_q, tm), col)],
        out_shape=[jax.ShapeDtypeStruct((N_TOK, D_MODEL), F32), jax.ShapeDtypeStruct((D_MODEL, N_TOK), BF16),
                   jax.ShapeDtypeStruct((n_q, N_TOK), F32)],
        compiler_params=_params("parallel"),
        name="mid",
    )(mo, uu, x, mod3, g2, wo1, wo2, wqt, keys)


N_CAND_ROWS = 80


def _top16(s, key_iota, sv_ref):
    rank = jnp.full(s.shape, PEER_TOPK, jnp.int32)
    for a in range(PEER_TOPK):
        m = jnp.max(s, axis=0, keepdims=True)
        first = jnp.min(jnp.where(s == m, key_iota, PEER_NKEYS), axis=0, keepdims=True)
        hit = key_iota == first
        rank = jnp.where(hit, a, rank)
        s = jnp.where(hit, -jnp.inf, s)
        sv_ref[a:a + 1, :] = m
    return rank


def _select_kernel(st_ref, pa_ref, pb_ref, sv1_ref, sv2_ref, cand_ref):
    key_iota = lax.broadcasted_iota(jnp.int32, (PEER_NKEYS, SEL_TILE), 0)
    crow = lax.broadcasted_iota(jnp.int32, (N_CAND_ROWS, SEL_TILE), 0)
    flat = jnp.where(crow < 16, crow * 16,
                     jnp.where(crow < 72, ((crow - 16) % 8) * 16 + (crow - 16) // 8 + 1, crow - 72 + 8))

    def head(h, carry):
        base = pl.multiple_of(h * 2 * PEER_NKEYS, 2 * PEER_NKEYS)
        s1 = st_ref[pl.ds(base, PEER_NKEYS), :]
        s2 = st_ref[pl.ds(base + PEER_NKEYS, PEER_NKEYS), :]
        r1 = _top16(s1, key_iota, sv1_ref)
        r2 = _top16(s2, key_iota, sv2_ref)
        top1 = sv1_ref[0:1, :]
        top2 = sv2_ref[0:1, :]
        cand_ref[0:16, :] = sv1_ref[...] + top2
        for b in range(1, 8):
            cand_ref[8 + 8 * b:16 + 8 * b, :] = sv1_ref[0:8, :] + sv2_ref[b:b + 1, :]
        cand_ref[72:80, :] = top1 + sv2_ref[8:16, :]
        cand = cand_ref[...]
        cmax = top1 + top2
        sel = jnp.zeros(cand.shape, jnp.bool_)
        z = jnp.zeros((1, SEL_TILE), F32)
        for _ in range(PEER_TOPK):
            m = jnp.max(cand, axis=0, keepdims=True)
            first = jnp.min(jnp.where(cand == m, flat, 256), axis=0, keepdims=True)
            hit = flat == first
            sel = sel | hit
            cand = jnp.where(hit, -jnp.inf, cand)
            z = z + jnp.exp(m - cmax)
        self32 = sel.astype(F32)
        counts = [jnp.sum(self32[0:16, :], axis=0, keepdims=True)]
        for b in range(1, 8):
            counts.append(jnp.sum(self32[8 + 8 * b:16 + 8 * b, :], axis=0, keepdims=True))
        for b in range(8, 16):
            counts.append(self32[72 + b - 8:72 + b - 7, :])
        n2 = jnp.zeros((PEER_NKEYS, SEL_TILE), F32)
        for b in range(PEER_TOPK):
            n2 = jnp.where(r2 == b, counts[b], n2)
        out = pl.multiple_of(h * SEL_ROWS_PER_HEAD, SEL_ROWS_PER_HEAD)
        pa_ref[pl.ds(out, PEER_NKEYS), :] = r1.astype(F32)
        pa_ref[pl.ds(out + PEER_NKEYS, PEER_NKEYS), :] = jnp.exp(s1 - top1)
        pb_ref[pl.ds(out, PEER_NKEYS), :] = n2.astype(BF16)
        pb_ref[pl.ds(out + PEER_NKEYS, PEER_NKEYS), :] = (jnp.exp(s2 - top2) / z).astype(BF16)
        return carry

    lax.fori_loop(0, PEER_HEADS, head, 0)


def _select(st):
    n_q = PEER_HEADS * PEER_DQ
    n_pe = PEER_HEADS * SEL_ROWS_PER_HEAD
    col = lambda i: (0, i)
    return pl.pallas_call(
        _select_kernel,
        grid=(N_TOK // SEL_TILE,),
        in_specs=[pl.BlockSpec((n_q, SEL_TILE), col)],
        out_specs=[pl.BlockSpec((n_pe, SEL_TILE), col), pl.BlockSpec((n_pe, SEL_TILE), col)],
        out_shape=[jax.ShapeDtypeStruct((n_pe, N_TOK), F32), jax.ShapeDtypeStruct((n_pe, N_TOK), BF16)],
        scratch_shapes=[pltpu.VMEM((PEER_TOPK, SEL_TILE), F32), pltpu.VMEM((PEER_TOPK, SEL_TILE), F32),
                        pltpu.VMEM((N_CAND_ROWS, SEL_TILE), F32)],
        compiler_params=_params("parallel"),
        name="select",
    )(st)


N_EXP_TILES = PEER_N // EXP_TILE
N_EXP_TOK_TILES = N_TOK // EXP_TOK_TILE
N_EXP_WORK = N_EXP_TOK_TILES * N_EXP_TILES
N_EXP_STEPS = N_EXP_WORK + 2


def _experts_kernel(h2t_ref, pa_ref, pb_ref, u_ref, vt_ref, x1_ref, mod_ref, gfin_ref, y_ref,
                    acc_ref, a_ref, c_ref):
    s = pl.program_id(0)
    item_gate = jnp.clip(s - 1, 0, N_EXP_WORK - 1)
    e_gate = item_gate % N_EXP_TILES
    e_out = jnp.maximum(s - 2, 0) % N_EXP_TILES
    n_lane_chunks = EXP_TOK_TILE // LANES
    tile3 = (PEER_NKEYS // BF16_ROWS, BF16_ROWS, LANES)

    @pl.when(s == 0)
    def _():
        a_ref[...] = jnp.zeros_like(a_ref)
        c_ref[...] = jnp.zeros_like(c_ref)

    @pl.when((s == 0) | ((s >= 2) & (e_out == 0)))
    def _():
        acc_ref[...] = jnp.zeros_like(acc_ref)

    acc_ref[...] += _dot(vt_ref[...], c_ref[...])

    for ib in range(EXP_TILE // PEER_NKEYS):
        i = e_gate * (EXP_TILE // PEER_NKEYS) + ib
        rows = slice(ib * PEER_NKEYS, (ib + 1) * PEER_NKEYS)
        rank_rows = [pa_ref[pl.ds(h * SEL_ROWS_PER_HEAD + i, 1), :] for h in range(PEER_HEADS)]
        e1_rows = [pa_ref[pl.ds(h * SEL_ROWS_PER_HEAD + PEER_NKEYS + i, 1), :] for h in range(PEER_HEADS)]
        for lc in range(n_lane_chunks):
            lanes = slice(lc * LANES, (lc + 1) * LANES)
            a = a_ref[rows, lanes]
            act = (0.5 * a * (1.0 + lax.erf(a * (2.0 ** -0.5)))).astype(BF16).reshape(tile3)
            w = jnp.zeros(tile3, BF16)
            for h in range(PEER_HEADS):
                base = h * SEL_ROWS_PER_HEAD
                rank_i = jnp.broadcast_to(rank_rows[h][:, lanes], (BF16_ROWS, LANES)).astype(BF16)
                e1_i = jnp.broadcast_to(e1_rows[h][:, lanes], (BF16_ROWS, LANES)).astype(BF16)
                n2 = pb_ref[base:base + PEER_NKEYS, lanes].reshape(tile3)
                e2 = pb_ref[base + PEER_NKEYS:base + 2 * PEER_NKEYS, lanes].reshape(tile3)
                w = w + jnp.where(rank_i[None] < n2, e1_i[None] * e2, jnp.zeros((), BF16))
            c_ref[rows, lanes] = (act * w).reshape(PEER_NKEYS, LANES)

    a_ref[...] = _dot(u_ref[...], h2t_ref[...])

    @pl.when((s >= 2) & (e_out == N_EXP_TILES - 1))
    def _():
        mod = mod_ref[0]
        gate2 = mod[:, 5 * D_MODEL:6 * D_MODEL]
        y = x1_ref[...] + gate2 * acc_ref[...].T
        ms = jnp.mean(y * y, axis=-1, keepdims=True)
        y_ref[...] = (y * lax.rsqrt(ms + EPS)) * gfin_ref[...]


def _experts(h2t, pa, pb, u_bf, vt_bf, x1, mod3, gfin):
    tm = EXP_TOK_TILE
    n_pe = PEER_HEADS * SEL_ROWS_PER_HEAD
    item = lambda s, lag: jnp.clip(s - lag, 0, N_EXP_WORK - 1)
    tok = lambda s, lag: item(s, lag) // N_EXP_TILES
    exp = lambda s, lag: item(s, lag) % N_EXP_TILES
    return pl.pallas_call(
        _experts_kernel,
        grid=(N_EXP_STEPS,),
        in_specs=[
            pl.BlockSpec((D_MODEL, tm), lambda s: (0, tok(s, 0))),
            pl.BlockSpec((n_pe, tm), lambda s: (0, tok(s, 1))),
            pl.BlockSpec((n_pe, tm), lambda s: (0, tok(s, 1))),
            pl.BlockSpec((EXP_TILE, D_MODEL), lambda s: (exp(s, 0), 0)),
            pl.BlockSpec((D_MODEL, EXP_TILE), lambda s: (0, exp(s, 2))),
            pl.BlockSpec((tm, D_MODEL), lambda s: (tok(s, 2), 0)),
            pl.BlockSpec((1, 1, 6 * D_MODEL), lambda s: (_mod_row(tok(s, 2), tm), 0, 0)),
            pl.BlockSpec((1, D_MODEL), lambda s: (0, 0)),
        ],
        out_specs=pl.BlockSpec((tm, D_MODEL), lambda s: (tok(s, 2), 0)),
        out_shape=jax.ShapeDtypeStruct((N_TOK, D_MODEL), F32),
        scratch_shapes=[pltpu.VMEM((D_MODEL, tm), F32), pltpu.VMEM((EXP_TILE, tm), F32),
                        pltpu.VMEM((EXP_TILE, tm), BF16)],
        compiler_params=_params("arbitrary"),
        name="experts",
    )(h2t, pa, pb, u_bf, vt_bf, x1, mod3, gfin)


def kernel(x_prompt, x_sample, c, state_gla_fwd, state_gla_bwd, c_ctx, norm1_g, w_mod, b_mod, w_in, w_af_up, b_af,
           w_ab_up, b_ab, gla_norm_g, conv_dw_w, conv_dw_b, conv_ln_g, conv_ln_b, conv_pw2, w_out, norm2_g, peer_wq,
           peer_subkeys, peer_u, peer_v, final_norm_g):
    depth = w_in.shape[0]
    assert depth == 1
    l = 0
    x = jnp.concatenate([x_prompt.reshape(N_CTX_TOK, D_MODEL), x_sample.reshape(N_LAT_TOK, D_MODEL)], axis=0)

    c_all = jnp.zeros((N_MOD_PAD, D_MODEL), F32).at[0].set(c_ctx).at[1:N_MOD_ROWS].set(c)
    w = w_in[l]
    w_main = jnp.concatenate([w[:, 0:1536], w[:, 1568:2592]], axis=1).astype(BF16)
    w_ab = jnp.zeros((D_MODEL, LANES), F32).at[:, 0:2 * GLA_RANK].set(w[:, 1536:1568]).astype(BF16)
    w_up = jnp.zeros((LANES, 2 * GLA_DK), F32)
    w_up = w_up.at[0:GLA_RANK, 0:GLA_DK].set(w_af_up[l]).at[GLA_RANK:2 * GLA_RANK, GLA_DK:].set(w_ab_up[l]).astype(BF16)
    b_up = jnp.concatenate([b_af[l], b_ab[l]])[None, :]
    conv_w = jnp.zeros((CONV_K + 1, CONV_CH), F32).at[0:CONV_K].set(conv_dw_w[l])
    wo1 = w_out[l][0:GLA_DV].astype(BF16)
    wo2 = w_out[l][GLA_DV:].astype(BF16)
    wqt = peer_wq[l].T.astype(BF16)
    keys = peer_subkeys[l].reshape(2 * PEER_HEADS, PEER_NKEYS, PEER_DQ // 2).astype(BF16)
    u_bf = peer_u[l].astype(BF16)
    vt_bf = peer_v[l].T.astype(BF16)

    mod = _modulation(c_all, w_mod[l], b_mod[l][None, :])
    mod3 = mod[:, None, :]

    q, k, v, gf, gb, sg, u = _mixin(x, mod3, norm1_g[l][None, :], w_main, w_ab, w_up, b_up)

    gn = gla_norm_g[l][None, :]
    mo0 = jnp.zeros((N_TOK, GLA_DV), BF16)
    mo1, new_f, new_b = _gla(q, k, v, gf, gb, sg, gn, mo0, seq_len=CTX_LEN, n_seq=N_CTX_SEQ, first_block=0,
                             init_states=None, emit_state=True)
    (mo,) = _gla(q, k, v, gf, gb, sg, gn, mo1, seq_len=LAT_LEN, n_seq=N_LAT_SEQ,
                 first_block=N_CTX_TOK // LAT_LEN, init_states=(state_gla_fwd[:, l], state_gla_bwd[:, l]),
                 emit_state=False)

    conv_args = (conv_w, conv_dw_b[l][None, :], conv_ln_g[l][None, :], conv_ln_b[l][None, :], conv_pw2[l].astype(BF16))
    uu0 = jnp.zeros((N_TOK, CONV_CH), BF16)
    uu1 = _conv(u, *conv_args, uu0, seq_len=CTX_LEN, n_seq=N_CTX_SEQ, first_block=0, latent=False)
    uu = _conv(u, *conv_args, uu1, seq_len=LAT_LEN, n_seq=N_LAT_SEQ, first_block=N_CTX_TOK // LAT_LEN, latent=True)

    x1, h2t, st = _mid(mo, uu, x, mod3, norm2_g[l][None, :], wo1, wo2, wqt, keys)
    pa, pb = _select(st)
    y = _experts(h2t, pa, pb, u_bf, vt_bf, x1, mod3, final_norm_g[None, :])

    y_prompt = y[0:N_CTX_TOK].reshape(N_CTX_SEQ, CTX_LEN, D_MODEL)
    y_sample = y[N_CTX_TOK:].reshape(N_LAT_SEQ, LAT_LEN, D_MODEL)
    return (y_prompt, y_sample, new_f[:, None], new_b[:, None])
```

```python
import functools

import jax
import jax.numpy as jnp
from jax import lax
from jax.experimental import pallas as pl
from jax.experimental.pallas import tpu as pltpu

F32 = jnp.float32
BF16 = jnp.bfloat16

D_MODEL = 1024
N_CTX_SEQ = 32
CTX_LEN = 256
N_LAT_SEQ = 8
LAT_LEN = 1024
N_CTX_TOK = N_CTX_SEQ * CTX_LEN
N_LAT_TOK = N_LAT_SEQ * LAT_LEN
N_TOK = N_CTX_TOK + N_LAT_TOK
GRID_W = 64
GRID_H = LAT_LEN // GRID_W
GLA_HEADS = 4
GLA_DK = 256
GLA_DV = 512
GLA_HK = 64
GLA_HV = 128
GLA_RANK = 16
GLA_NORMALIZER = 16.0
CHUNK = 64
CONV_CH = 512
CONV_K = 31
CONV_PAD = CONV_K // 2
PEER_HEADS = 8
PEER_NKEYS = 128
PEER_N = PEER_NKEYS * PEER_NKEYS
PEER_DQ = 256
PEER_TOPK = 16
EPS = 1e-6
N_MOD_ROWS = 1 + N_LAT_SEQ
N_MOD_PAD = 16

LANES = 128
SUBLANES = 8
VMEM_LIMIT_BYTES = 56 * 1024 * 1024

TOK_TILE = 256
SEL_TILE = LANES
EXP_TOK_TILE = 512
EXP_TILE = 1024
EXP_SUB = 256
SEL_ROWS_PER_HEAD = 2 * PEER_NKEYS
BF16_ROWS = 16


def _params(*semantics):
    return pltpu.CompilerParams(dimension_semantics=semantics, vmem_limit_bytes=VMEM_LIMIT_BYTES)


def _mod_row(tile, tile_rows):
    first = tile * tile_rows
    return jnp.where(first < N_CTX_TOK, 0, 1 + (first - N_CTX_TOK) // LAT_LEN)


def _ctx_spec(tile_rows):
    n_ctx_tiles = N_CTX_TOK // tile_rows
    return pl.BlockSpec((tile_rows, D_MODEL), lambda i: (jnp.minimum(i, n_ctx_tiles - 1), 0))


def _lat_spec(tile_rows):
    n_ctx_tiles = N_CTX_TOK // tile_rows
    return pl.BlockSpec((tile_rows, D_MODEL), lambda i: (jnp.maximum(i - n_ctx_tiles, 0), 0))


def _token_tile(xp_ref, xs_ref, tile_rows):
    is_ctx = pl.program_id(0) < N_CTX_TOK // tile_rows
    return jnp.where(is_ctx, xp_ref[...], xs_ref[...])


def _split_bf16(a):
    hi = a.astype(BF16)
    lo = (a - hi.astype(F32)).astype(BF16)
    return hi, lo


def _dot(a, b):
    return jnp.dot(a, b, preferred_element_type=F32)


def _dot_nt(a, b):
    return lax.dot_general(a, b, (((1,), (1,)), ((), ())), preferred_element_type=F32)


def _dot_tn(a, b):
    return lax.dot_general(a, b, (((0,), (0,)), ((), ())), preferred_element_type=F32)


def _mod_kernel(c_ref, w_ref, b_ref, o_ref):
    c = c_ref[...]
    a = c * jax.nn.sigmoid(c)
    a_hi, a_lo = _split_bf16(a)
    w_hi, w_lo = _split_bf16(w_ref[...])
    o_ref[...] = _dot(a_hi, w_hi) + _dot(a_hi, w_lo) + _dot(a_lo, w_hi) + b_ref[...]


def _modulation(c_all, w_mod, b_mod):
    n_out = w_mod.shape[1]
    tile = 1024
    return pl.pallas_call(
        _mod_kernel,
        grid=(n_out // tile,),
        in_specs=[
            pl.BlockSpec((N_MOD_PAD, D_MODEL), lambda j: (0, 0)),
            pl.BlockSpec((D_MODEL, tile), lambda j: (0, j)),
            pl.BlockSpec((1, tile), lambda j: (0, j)),
        ],
        out_specs=pl.BlockSpec((N_MOD_PAD, tile), lambda j: (0, j)),
        out_shape=jax.ShapeDtypeStruct((N_MOD_PAD, n_out), F32),
        compiler_params=_params("parallel"),
        name="mod",
    )(c_all, w_mod, b_mod)


N_MAIN = 2 * GLA_DK + 2 * GLA_DV + 2 * CONV_CH


def _mixin_kernel(xp_ref, xs_ref, mod_ref, g1_ref, wm_ref, wab_ref, wup_ref, bup_ref,
                  q_ref, k_ref, v_ref, gf_ref, gb_ref, sg_ref, u_ref):
    x = _token_tile(xp_ref, xs_ref, TOK_TILE)
    mod = mod_ref[0]
    shift1 = mod[:, 0:D_MODEL]
    scale1 = mod[:, D_MODEL:2 * D_MODEL]
    ms = jnp.mean(x * x, axis=-1, keepdims=True)
    h = (x * lax.rsqrt(ms + EPS)) * g1_ref[...]
    h = h * (1.0 + scale1) + shift1
    hb = h.astype(BF16)
    p = _dot(hb, wm_ref[...])
    q_ref[...] = p[:, 0:256] * (GLA_HK ** -0.5)
    k_ref[...] = p[:, 256:512]
    v_ref[...] = p[:, 512:1024]
    g = p[:, 1024:1536]
    sg_ref[...] = g * jax.nn.sigmoid(g)
    ca = p[:, 1536:2048]
    cb = p[:, 2048:2560]
    u_ref[...] = ca * jax.nn.sigmoid(cb)
    a = _dot(hb, wab_ref[...])
    z = _dot(a.astype(BF16), wup_ref[...]) + bup_ref[...]
    ls = jnp.minimum(z, 0.0) - jnp.log(1.0 + jnp.exp(-jnp.abs(z)))
    gl = ls * (1.0 / GLA_NORMALIZER)
    gf_ref[...] = gl[:, 0:GLA_DK]
    gb_ref[...] = gl[:, GLA_DK:2 * GLA_DK]


def _mixin(xp, xs, mod3, g1, w_main, w_ab, w_up, b_up):
    tm = TOK_TILE
    row = lambda i: (i, 0)
    const = lambda i: (0, 0)
    outs = [(GLA_DK, F32), (GLA_DK, F32), (GLA_DV, F32), (GLA_DK, F32), (GLA_DK, F32),
            (GLA_DV, F32), (CONV_CH, F32)]
    return pl.pallas_call(
        _mixin_kernel,
        grid=(N_TOK // tm,),
        in_specs=[
            _ctx_spec(tm), _lat_spec(tm),
            pl.BlockSpec((1, 1, 6 * D_MODEL), lambda i: (_mod_row(i, tm), 0, 0)),
            pl.BlockSpec((1, D_MODEL), const),
            pl.BlockSpec((D_MODEL, N_MAIN), const),
            pl.BlockSpec((D_MODEL, LANES), const),
            pl.BlockSpec((LANES, 2 * GLA_DK), const),
            pl.BlockSpec((1, 2 * GLA_DK), const),
        ],
        out_specs=[pl.BlockSpec((tm, n), row) for n, _ in outs],
        out_shape=[jax.ShapeDtypeStruct((N_TOK, n), dt) for n, dt in outs],
        compiler_params=_params("parallel"),
        name="mixin",
    )(xp, xs, mod3, g1, w_main, w_ab, w_up, b_up)


def _gla_kernel(*refs, seq_len, has_init, emit_state):
    q_ref, k_ref, v_ref, gf_ref, gb_ref, sg_ref, gn_ref = refs[:7]
    pos = 7
    if has_init:
        s0f_ref, s0b_ref = refs[pos:pos + 2]
        pos += 2
    pos += 1
    o_ref = refs[pos]
    pos += 1
    if emit_state:
        sf_ref, sb_ref = refs[pos:pos + 2]
        pos += 2
    oacc_ref, stf_ref, stb_ref = refs[pos:pos + 3]

    n_chunks = seq_len // CHUNK
    rows_i = lax.broadcasted_iota(jnp.int32, (CHUNK, CHUNK), 0)
    cols_i = lax.broadcasted_iota(jnp.int32, (CHUNK, CHUNK), 1)
    lane = lax.broadcasted_iota(jnp.int32, (1, GLA_DK), 1)
    head_mask = [(lane >= h * GLA_HK) & (lane < (h + 1) * GLA_HK) for h in range(GLA_HEADS)]
    srow = lax.broadcasted_iota(jnp.int32, (GLA_HEADS * CHUNK, CHUNK), 0) % CHUNK
    scol = lax.broadcasted_iota(jnp.int32, (GLA_HEADS * CHUNK, CHUNK), 1)
    ones_t = jnp.ones((CHUNK, GLA_HV), BF16)

    def chunk(c, g_ref, st_ref, forward):
        if forward:
            tri = (rows_i >= cols_i).astype(BF16)
            causal = srow >= scol
            ref_row, last_row = CHUNK // 2 - 1, CHUNK - 1
        else:
            tri = (rows_i <= cols_i).astype(BF16)
            causal = srow <= scol
            ref_row, last_row = CHUNK // 2, 0
        r0 = pl.multiple_of(c * CHUNK, CHUNK)
        rows = pl.ds(r0, CHUNK)
        g = g_ref[rows, :]
        q = q_ref[rows, :]
        k = k_ref[rows, :]
        vb = v_ref[rows, :].astype(BF16)
        g_hi, g_lo = _split_bf16(g)
        b = _dot(tri, g_hi) + _dot(tri, g_lo)
        b_mid = b[ref_row:ref_row + 1, :]
        b_end = b[last_row:last_row + 1, :]
        qt = q * jnp.exp(b - b_mid)
        kt = (k * jnp.exp(b_mid - b)).astype(BF16)
        qe = q * jnp.exp(b)
        kd = (k * jnp.exp(b_end - b)).astype(BF16)
        q_stack = jnp.concatenate([jnp.where(m, qt, 0.0) for m in head_mask], axis=0).astype(BF16)
        qe_stack = jnp.concatenate([jnp.where(m, qe, 0.0) for m in head_mask], axis=0).astype(BF16)
        scores = jnp.where(causal, _dot_nt(q_stack, kt), 0.0)
        o_intra = _dot(scores.astype(BF16), vb)
        state = st_ref[...]
        o_inter = _dot(qe_stack, state.astype(BF16))
        upd = _dot_tn(kd, vb)
        dec = jnp.exp(_dot_tn(g_hi, ones_t) + _dot_tn(g_lo, ones_t))
        for h in range(GLA_HEADS):
            hr = slice(h * CHUNK, (h + 1) * CHUNK)
            hv = slice(h * GLA_HV, (h + 1) * GLA_HV)
            oacc_ref[rows, hv] += o_intra[hr, hv] + o_inter[hr, :]
            st_ref[hr, :] = dec[hr, :] * state[hr, :] + upd[hr, hv]

    oacc_ref[...] = jnp.zeros_like(oacc_ref)
    if has_init:
        stf_ref[...] = s0f_ref[0].reshape(GLA_HEADS * GLA_HK, GLA_HV)
        stb_ref[...] = s0b_ref[0].reshape(GLA_HEADS * GLA_HK, GLA_HV)
    else:
        stf_ref[...] = jnp.zeros_like(stf_ref)
        stb_ref[...] = jnp.zeros_like(stb_ref)

    def step(it, carry):
        chunk(it, gf_ref, stf_ref, True)
        chunk(n_chunks - 1 - it, gb_ref, stb_ref, False)
        return carry

    lax.fori_loop(0, n_chunks, step, 0)
    if emit_state:
        sf_ref[0] = stf_ref[...].reshape(GLA_HEADS, GLA_HK, GLA_HV)
        sb_ref[0] = stb_ref[...].reshape(GLA_HEADS, GLA_HK, GLA_HV)

    o = oacc_ref[...]
    gn = gn_ref[...]
    sg = sg_ref[...]
    for h in range(GLA_HEADS):
        hv = slice(h * GLA_HV, (h + 1) * GLA_HV)
        oh = o[:, hv]
        ms = jnp.mean(oh * oh, axis=-1, keepdims=True)
        o_ref[:, hv] = ((oh * lax.rsqrt(ms + EPS)) * gn[:, hv] * sg[:, hv]).astype(o_ref.dtype)


def _gla(q, k, v, gf, gb, sg, gn, prev_out, *, seq_len, n_seq, first_block, init_states, emit_state):
    row = lambda i: (first_block + i, 0)
    const = lambda i: (0, 0)
    st_spec = pl.BlockSpec((1, GLA_HEADS, GLA_HK, GLA_HV), lambda i: (i, 0, 0, 0))
    in_specs = [
        pl.BlockSpec((seq_len, GLA_DK), row), pl.BlockSpec((seq_len, GLA_DK), row),
        pl.BlockSpec((seq_len, GLA_DV), row), pl.BlockSpec((seq_len, GLA_DK), row),
        pl.BlockSpec((seq_len, GLA_DK), row), pl.BlockSpec((seq_len, GLA_DV), row),
        pl.BlockSpec((1, GLA_DV), const),
    ]
    args = [q, k, v, gf, gb, sg, gn]
    if init_states is not None:
        in_specs += [st_spec, st_spec]
        args += list(init_states)
    in_specs.append(pl.BlockSpec(memory_space=pl.ANY))
    args.append(prev_out)
    out_specs = [pl.BlockSpec((seq_len, GLA_DV), row)]
    out_shape = [jax.ShapeDtypeStruct(prev_out.shape, prev_out.dtype)]
    if emit_state:
        out_specs += [st_spec, st_spec]
        out_shape += [jax.ShapeDtypeStruct((n_seq, GLA_HEADS, GLA_HK, GLA_HV), F32)] * 2
    return pl.pallas_call(
        functools.partial(_gla_kernel, seq_len=seq_len, has_init=init_states is not None, emit_state=emit_state),
        grid=(n_seq,),
        in_specs=in_specs,
        out_specs=out_specs,
        out_shape=out_shape,
        scratch_shapes=[pltpu.VMEM((seq_len, GLA_DV), F32), pltpu.VMEM((GLA_HEADS * GLA_HK, GLA_HV), F32),
                        pltpu.VMEM((GLA_HEADS * GLA_HK, GLA_HV), F32)],
        input_output_aliases={len(args) - 1: 0},
        compiler_params=_params("parallel"),
        name="gla_lat" if init_states is not None else "gla_ctx",
    )(*args)


ROW_PITCH = GRID_W + 2 * (CONV_PAD + 1)


def _conv_kernel(u_ref, w_ref, b_ref, lng_ref, lnb_ref, pw_ref, prev_ref, o_ref, pad_ref, y_ref, *, latent, seq_len):
    del prev_ref
    half = CONV_CH // 2
    blk = GRID_W
    n_blk = seq_len // blk
    n_pad = pad_ref.shape[1]
    pad_ref[0] = jnp.zeros(pad_ref.shape[1:], F32)
    if latent:
        for r in range(n_blk):
            pad_ref[0, r * ROW_PITCH + CONV_PAD + 1:r * ROW_PITCH + CONV_PAD + 1 + blk, :] = u_ref[r * blk:(r + 1) * blk, 0:half]
    else:
        pad_ref[0, CONV_PAD + 1:CONV_PAD + 1 + seq_len, :] = u_ref[...]
    for s in range(1, SUBLANES):
        pad_ref[s, 0:n_pad - SUBLANES, :] = pad_ref[0, s:n_pad - SUBLANES + s, :]

    def window(first_row, tap):
        q, s = divmod(tap + 1, SUBLANES)
        return pad_ref[s, first_row + q * SUBLANES:first_row + q * SUBLANES + blk, :]

    bias = b_ref[...]
    lng = lng_ref[...]
    lnb = lnb_ref[...]
    for r in range(n_blk):
        if latent:
            acc_w = jnp.zeros((blk, half), F32)
            for t in range(CONV_K):
                acc_w += w_ref[t:t + 1, 0:half] * window(r * ROW_PITCH, t)
            acc_h = jnp.zeros((blk, half), F32)
            for r2 in range(n_blk):
                t = r2 - r + CONV_PAD
                acc_h += w_ref[t:t + 1, half:CONV_CH] * u_ref[r2 * blk:(r2 + 1) * blk, half:CONV_CH]
            y = jnp.concatenate([acc_w, acc_h], axis=-1) + bias
        else:
            acc = jnp.zeros((blk, CONV_CH), F32)
            for t in range(CONV_K):
                acc += w_ref[t:t + 1, :] * window(r * blk, t)
            y = acc + bias
        mu = jnp.mean(y, axis=-1, keepdims=True)
        yc = y - mu
        var = jnp.mean(yc * yc, axis=-1, keepdims=True)
        yn = (yc * lax.rsqrt(var + EPS)) * lng + lnb
        y_ref[r * blk:(r + 1) * blk, :] = (yn * jax.nn.sigmoid(yn)).astype(BF16)
    o_ref[...] = _dot(y_ref[...], pw_ref[...]).astype(o_ref.dtype)


def _conv(u, w, b, lng, lnb, pw, prev_out, *, seq_len, n_seq, first_block, latent):
    row = lambda i: (first_block + i, 0)
    const = lambda i: (0, 0)
    if latent:
        pad_shape = (SUBLANES, GRID_H * ROW_PITCH, CONV_CH // 2)
    else:
        pad_shape = (SUBLANES, seq_len + 2 * (CONV_PAD + 1), CONV_CH)
    return pl.pallas_call(
        functools.partial(_conv_kernel, latent=latent, seq_len=seq_len),
        grid=(n_seq,),
        in_specs=[
            pl.BlockSpec((seq_len, CONV_CH), row),
            pl.BlockSpec((CONV_K + 1, CONV_CH), const),
            pl.BlockSpec((1, CONV_CH), const), pl.BlockSpec((1, CONV_CH), const), pl.BlockSpec((1, CONV_CH), const),
            pl.BlockSpec((CONV_CH, CONV_CH), const),
            pl.BlockSpec(memory_space=pl.ANY),
        ],
        out_specs=pl.BlockSpec((seq_len, CONV_CH), row),
        out_shape=jax.ShapeDtypeStruct(prev_out.shape, prev_out.dtype),
        scratch_shapes=[pltpu.VMEM(pad_shape, F32), pltpu.VMEM((seq_len, CONV_CH), BF16)],
        input_output_aliases={6: 0},
        compiler_params=_params("parallel"),
        name="conv_lat" if latent else "conv_ctx",
    )(u, w, b, lng, lnb, pw, prev_out)


def _mid_kernel(mo_ref, uu_ref, xp_ref, xs_ref, mod_ref, g2_ref, wo1_ref, wo2_ref, wqt_ref, keys_ref,
                x1_ref, h2t_ref, st_ref):
    mod = mod_ref[0]
    gate1 = mod[:, 2 * D_MODEL:3 * D_MODEL]
    shift2 = mod[:, 3 * D_MODEL:4 * D_MODEL]
    scale2 = mod[:, 4 * D_MODEL:5 * D_MODEL]
    m = _dot(mo_ref[...], wo1_ref[...]) + _dot(uu_ref[...], wo2_ref[...])
    x1 = _token_tile(xp_ref, xs_ref, TOK_TILE) + gate1 * m
    x1_ref[...] = x1
    ms = jnp.mean(x1 * x1, axis=-1, keepdims=True)
    h2 = (x1 * lax.rsqrt(ms + EPS)) * g2_ref[...]
    h2 = h2 * (1.0 + scale2) + shift2
    h2t = h2.T.astype(BF16)
    h2t_ref[...] = h2t
    qt = _dot(wqt_ref[...], h2t)
    for hp in range(2 * PEER_HEADS):
        rows = slice(hp * PEER_NKEYS, (hp + 1) * PEER_NKEYS)
        st_ref[rows, :] = _dot(keys_ref[hp], qt[rows, :].astype(BF16))


def _mid(mo, uu, xp, xs, mod3, g2, wo1, wo2, wqt, keys):
    tm = TOK_TILE
    row = lambda i: (i, 0)
    col = lambda i: (0, i)
    const = lambda i: (0, 0)
    n_q = PEER_HEADS * PEER_DQ
    return pl.pallas_call(
        _mid_kernel,
        grid=(N_TOK // tm,),
        in_specs=[
            pl.BlockSpec((tm, GLA_DV), row), pl.BlockSpec((tm, CONV_CH), row), _ctx_spec(tm), _lat_spec(tm),
            pl.BlockSpec((1, 1, 6 * D_MODEL), lambda i: (_mod_row(i, tm), 0, 0)),
            pl.BlockSpec((1, D_MODEL), const),
            pl.BlockSpec((GLA_DV, D_MODEL), const), pl.BlockSpec((CONV_CH, D_MODEL), const),
            pl.BlockSpec((n_q, D_MODEL), const),
            pl.BlockSpec((2 * PEER_HEADS, PEER_NKEYS, PEER_DQ // 2), lambda i: (0, 0, 0)),
        ],
        out_specs=[pl.BlockSpec((tm, D_MODEL), row), pl.BlockSpec((D_MODEL, tm), col), pl.BlockSpec((n_q, tm), col)],
        out_shape=[jax.ShapeDtypeStruct((N_TOK, D_MODEL), F32), jax.ShapeDtypeStruct((D_MODEL, N_TOK), BF16),
                   jax.ShapeDtypeStruct((n_q, N_TOK), F32)],
        compiler_params=_params("parallel"),
        name="mid",
    )(mo, uu, xp, xs, mod3, g2, wo1, wo2, wqt, keys)


N_CAND_ROWS = 80


def _top16(s, key_iota, sv_ref, exact):
    rank = jnp.full(s.shape, PEER_TOPK, jnp.int32)
    for a in range(PEER_TOPK):
        m = jnp.max(s, axis=0, keepdims=True)
        hit = s == m
        if exact:
            first = jnp.min(jnp.where(hit, key_iota, PEER_NKEYS), axis=0, keepdims=True)
            hit = key_iota == first
        rank = jnp.where(hit, a, rank)
        s = jnp.where(hit, -jnp.inf, s)
        sv_ref[a:a + 1, :] = m
    return rank


def _count_true(mask):
    return jnp.sum(jnp.where(mask, 1.0, 0.0), axis=0, keepdims=True)


def _select_head(h, bad, st_ref, pa_ref, pb_ref, sv1_ref, sv2_ref, cand_ref, *, exact):
    key_iota = lax.broadcasted_iota(jnp.int32, (PEER_NKEYS, SEL_TILE), 0)
    crow = lax.broadcasted_iota(jnp.int32, (N_CAND_ROWS, SEL_TILE), 0)
    flat = jnp.where(crow < 16, crow * 16,
                     jnp.where(crow < 72, ((crow - 16) % 8) * 16 + (crow - 16) // 8 + 1, crow - 72 + 8))
    base = pl.multiple_of(h * 2 * PEER_NKEYS, 2 * PEER_NKEYS)
    s1 = st_ref[pl.ds(base, PEER_NKEYS), :]
    s2 = st_ref[pl.ds(base + PEER_NKEYS, PEER_NKEYS), :]
    r1 = _top16(s1, key_iota, sv1_ref, exact)
    r2 = _top16(s2, key_iota, sv2_ref, exact)
    top1 = sv1_ref[0:1, :]
    top2 = sv2_ref[0:1, :]
    cand_ref[0:16, :] = sv1_ref[...] + top2
    for b in range(1, 8):
        cand_ref[8 + 8 * b:16 + 8 * b, :] = sv1_ref[0:8, :] + sv2_ref[b:b + 1, :]
    cand_ref[72:80, :] = top1 + sv2_ref[8:16, :]
    cand = cand_ref[...]
    cmax = top1 + top2
    sel = jnp.zeros(cand.shape, jnp.bool_)
    z = jnp.zeros((1, SEL_TILE), F32)
    for _ in range(PEER_TOPK):
        m = jnp.max(cand, axis=0, keepdims=True)
        hit = cand == m
        if exact:
            first = jnp.min(jnp.where(hit, flat, 256), axis=0, keepdims=True)
            hit = flat == first
        sel = sel | hit
        cand = jnp.where(hit, -jnp.inf, cand)
        z = z + jnp.exp(m - cmax)
    self32 = jnp.where(sel, 1.0, 0.0)
    counts = [jnp.sum(self32[0:16, :], axis=0, keepdims=True)]
    for b in range(1, 8):
        counts.append(jnp.sum(self32[8 + 8 * b:16 + 8 * b, :], axis=0, keepdims=True))
    for b in range(8, 16):
        counts.append(self32[72 + b - 8:72 + b - 7, :])
    n2 = jnp.zeros((PEER_NKEYS, SEL_TILE), F32)
    for b in range(PEER_TOPK):
        n2 = jnp.where(r2 == b, counts[b], n2)
    out = pl.multiple_of(h * SEL_ROWS_PER_HEAD, SEL_ROWS_PER_HEAD)
    pa_ref[pl.ds(out, PEER_NKEYS), :] = r1.astype(F32)
    pa_ref[pl.ds(out + PEER_NKEYS, PEER_NKEYS), :] = jnp.exp(s1 - top1)
    pb_ref[pl.ds(out, PEER_NKEYS), :] = n2.astype(BF16)
    pb_ref[pl.ds(out + PEER_NKEYS, PEER_NKEYS), :] = (jnp.exp(s2 - top2) / z).astype(BF16)
    if exact:
        return bad
    n_taken = _count_true(r1 < PEER_TOPK) + _count_true(r2 < PEER_TOPK) + jnp.sum(self32, axis=0, keepdims=True)
    return jnp.maximum(bad, jnp.where(n_taken == 3.0 * PEER_TOPK, 0.0, 1.0))


HEADS_PER_TRIP = 2


def _select_kernel(st_ref, pa_ref, pb_ref, sv1_ref, sv2_ref, cand_ref):
    def trip(exact):
        def body(t, bad):
            for k in range(HEADS_PER_TRIP):
                bad = _select_head(t * HEADS_PER_TRIP + k, bad, st_ref, pa_ref, pb_ref,
                                   sv1_ref.at[k], sv2_ref.at[k], cand_ref.at[k], exact=exact)
            return bad
        return body

    bad = lax.fori_loop(0, PEER_HEADS // HEADS_PER_TRIP, trip(False), jnp.zeros((1, SEL_TILE), F32))

    @pl.when(jnp.max(bad) > 0.0)
    def _():
        lax.fori_loop(0, PEER_HEADS // HEADS_PER_TRIP, trip(True), bad)


def _select(st):
    n_q = PEER_HEADS * PEER_DQ
    n_pe = PEER_HEADS * SEL_ROWS_PER_HEAD
    col = lambda i: (0, i)
    return pl.pallas_call(
        _select_kernel,
        grid=(N_TOK // SEL_TILE,),
        in_specs=[pl.BlockSpec((n_q, SEL_TILE), col)],
        out_specs=[pl.BlockSpec((n_pe, SEL_TILE), col), pl.BlockSpec((n_pe, SEL_TILE), col)],
        out_shape=[jax.ShapeDtypeStruct((n_pe, N_TOK), F32), jax.ShapeDtypeStruct((n_pe, N_TOK), BF16)],
        scratch_shapes=[pltpu.VMEM((HEADS_PER_TRIP, PEER_TOPK, SEL_TILE), F32),
                        pltpu.VMEM((HEADS_PER_TRIP, PEER_TOPK, SEL_TILE), F32),
                        pltpu.VMEM((HEADS_PER_TRIP, N_CAND_ROWS, SEL_TILE), F32)],
        compiler_params=_params("parallel"),
        name="select",
    )(st)


N_EXP_TILES = PEER_N // EXP_TILE
N_EXP_TOK_TILES = N_TOK // EXP_TOK_TILE
N_EXP_WORK = N_EXP_TOK_TILES * N_EXP_TILES
N_EXP_STEPS = N_EXP_WORK + 2


def _experts_kernel(h2t_ref, pa_ref, pb_ref, u_ref, vt_ref, x1_ref, mod_ref, gfin_ref, yp_ref, ys_ref,
                    acc_ref, a_ref, c_ref):
    s = pl.program_id(0)
    item_gate = jnp.clip(s - 1, 0, N_EXP_WORK - 1)
    e_gate = item_gate % N_EXP_TILES
    e_out = jnp.maximum(s - 2, 0) % N_EXP_TILES
    n_lane_chunks = EXP_TOK_TILE // LANES
    tile3 = (PEER_NKEYS // BF16_ROWS, BF16_ROWS, LANES)

    @pl.when(s == 0)
    def _():
        a_ref[...] = jnp.zeros_like(a_ref)
        c_ref[...] = jnp.zeros_like(c_ref)

    @pl.when((s == 0) | ((s >= 2) & (e_out == 0)))
    def _():
        acc_ref[...] = jnp.zeros_like(acc_ref)

    acc_ref[...] += _dot(vt_ref[...], c_ref[...])

    for ib in range(EXP_TILE // PEER_NKEYS):
        i = e_gate * (EXP_TILE // PEER_NKEYS) + ib
        rows = slice(ib * PEER_NKEYS, (ib + 1) * PEER_NKEYS)
        rank_rows = [pa_ref[pl.ds(h * SEL_ROWS_PER_HEAD + i, 1), :] for h in range(PEER_HEADS)]
        e1_rows = [pa_ref[pl.ds(h * SEL_ROWS_PER_HEAD + PEER_NKEYS + i, 1), :] for h in range(PEER_HEADS)]
        for lc in range(n_lane_chunks):
            lanes = slice(lc * LANES, (lc + 1) * LANES)
            a = a_ref[rows, lanes]
            act = (0.5 * a * (1.0 + lax.erf(a * (2.0 ** -0.5)))).astype(BF16).reshape(tile3)
            w = jnp.zeros(tile3, BF16)
            for h in range(PEER_HEADS):
                base = h * SEL_ROWS_PER_HEAD
                rank_i = jnp.broadcast_to(rank_rows[h][:, lanes], (BF16_ROWS, LANES)).astype(BF16)
                e1_i = jnp.broadcast_to(e1_rows[h][:, lanes], (BF16_ROWS, LANES)).astype(BF16)
                n2 = pb_ref[base:base + PEER_NKEYS, lanes].reshape(tile3)
                e2 = pb_ref[base + PEER_NKEYS:base + 2 * PEER_NKEYS, lanes].reshape(tile3)
                w = w + jnp.where(rank_i[None] < n2, e1_i[None] * e2, jnp.zeros((), BF16))
            c_ref[rows, lanes] = (act * w).reshape(PEER_NKEYS, LANES)

    a_ref[...] = _dot(u_ref[...], h2t_ref[...])

    @pl.when((s >= 2) & (e_out == N_EXP_TILES - 1))
    def _():
        mod = mod_ref[0]
        gate2 = mod[:, 5 * D_MODEL:6 * D_MODEL]
        y = x1_ref[...] + gate2 * acc_ref[...].T
        ms = jnp.mean(y * y, axis=-1, keepdims=True)
        y = (y * lax.rsqrt(ms + EPS)) * gfin_ref[...]
        is_ctx = jnp.maximum(s - 2, 0) // N_EXP_TILES < N_CTX_TOK // EXP_TOK_TILE

        @pl.when(is_ctx)
        def _():
            yp_ref[...] = y

        @pl.when(jnp.logical_not(is_ctx))
        def _():
            ys_ref[...] = y


def _experts(h2t, pa, pb, u_bf, vt_bf, x1, mod3, gfin):
    tm = EXP_TOK_TILE
    n_pe = PEER_HEADS * SEL_ROWS_PER_HEAD
    item = lambda s, lag: jnp.clip(s - lag, 0, N_EXP_WORK - 1)
    tok = lambda s, lag: item(s, lag) // N_EXP_TILES
    exp = lambda s, lag: item(s, lag) % N_EXP_TILES
    n_ctx_tiles = N_CTX_TOK // tm
    return pl.pallas_call(
        _experts_kernel,
        grid=(N_EXP_STEPS,),
        in_specs=[
            pl.BlockSpec((D_MODEL, tm), lambda s: (0, tok(s, 0))),
            pl.BlockSpec((n_pe, tm), lambda s: (0, tok(s, 1))),
            pl.BlockSpec((n_pe, tm), lambda s: (0, tok(s, 1))),
            pl.BlockSpec((EXP_TILE, D_MODEL), lambda s: (exp(s, 0), 0)),
            pl.BlockSpec((D_MODEL, EXP_TILE), lambda s: (0, exp(s, 2))),
            pl.BlockSpec((tm, D_MODEL), lambda s: (tok(s, 2), 0)),
            pl.BlockSpec((1, 1, 6 * D_MODEL), lambda s: (_mod_row(tok(s, 2), tm), 0, 0)),
            pl.BlockSpec((1, D_MODEL), lambda s: (0, 0)),
        ],
        out_specs=[pl.BlockSpec((tm, D_MODEL), lambda s: (jnp.minimum(tok(s, 2), n_ctx_tiles - 1), 0)),
                   pl.BlockSpec((tm, D_MODEL), lambda s: (jnp.maximum(tok(s, 2) - n_ctx_tiles, 0), 0))],
        out_shape=[jax.ShapeDtypeStruct((N_CTX_TOK, D_MODEL), F32), jax.ShapeDtypeStruct((N_LAT_TOK, D_MODEL), F32)],
        scratch_shapes=[pltpu.VMEM((D_MODEL, tm), F32), pltpu.VMEM((EXP_TILE, tm), F32),
                        pltpu.VMEM((EXP_TILE, tm), BF16)],
        compiler_params=_params("arbitrary"),
        name="experts",
    )(h2t, pa, pb, u_bf, vt_bf, x1, mod3, gfin)


def kernel(x_prompt, x_sample, c, state_gla_fwd, state_gla_bwd, c_ctx, norm1_g, w_mod, b_mod, w_in, w_af_up, b_af,
           w_ab_up, b_ab, gla_norm_g, conv_dw_w, conv_dw_b, conv_ln_g, conv_ln_b, conv_pw2, w_out, norm2_g, peer_wq,
           peer_subkeys, peer_u, peer_v, final_norm_g):
    depth = w_in.shape[0]
    assert depth == 1
    l = 0
    xp = x_prompt.reshape(N_CTX_TOK, D_MODEL)
    xs = x_sample.reshape(N_LAT_TOK, D_MODEL)

    c_all = jnp.zeros((N_MOD_PAD, D_MODEL), F32).at[0].set(c_ctx).at[1:N_MOD_ROWS].set(c)
    w = w_in[l]
    w_main = jnp.concatenate([w[:, 0:1536], w[:, 1568:2592]], axis=1).astype(BF16)
    w_ab = jnp.zeros((D_MODEL, LANES), F32).at[:, 0:2 * GLA_RANK].set(w[:, 1536:1568]).astype(BF16)
    w_up = jnp.zeros((LANES, 2 * GLA_DK), F32)
    w_up = w_up.at[0:GLA_RANK, 0:GLA_DK].set(w_af_up[l]).at[GLA_RANK:2 * GLA_RANK, GLA_DK:].set(w_ab_up[l]).astype(BF16)
    b_up = jnp.concatenate([b_af[l], b_ab[l]])[None, :]
    conv_w = jnp.zeros((CONV_K + 1, CONV_CH), F32).at[0:CONV_K].set(conv_dw_w[l])
    wo1 = w_out[l][0:GLA_DV].astype(BF16)
    wo2 = w_out[l][GLA_DV:].astype(BF16)
    wqt = peer_wq[l].T.astype(BF16)
    keys = peer_subkeys[l].reshape(2 * PEER_HEADS, PEER_NKEYS, PEER_DQ // 2).astype(BF16)
    u_bf = peer_u[l].astype(BF16)
    vt_bf = peer_v[l].T.astype(BF16)

    mod = _modulation(c_all, w_mod[l], b_mod[l][None, :])
    mod3 = mod[:, None, :]

    q, k, v, gf, gb, sg, u = _mixin(xp, xs, mod3, norm1_g[l][None, :], w_main, w_ab, w_up, b_up)

    gn = gla_norm_g[l][None, :]
    mo0 = jnp.zeros((N_TOK, GLA_DV), BF16)
    mo1, new_f, new_b = _gla(q, k, v, gf, gb, sg, gn, mo0, seq_len=CTX_LEN, n_seq=N_CTX_SEQ, first_block=0,
                             init_states=None, emit_state=True)
    (mo,) = _gla(q, k, v, gf, gb, sg, gn, mo1, seq_len=LAT_LEN, n_seq=N_LAT_SEQ,
                 first_block=N_CTX_TOK // LAT_LEN, init_states=(state_gla_fwd[:, l], state_gla_bwd[:, l]),
                 emit_state=False)

    conv_args = (conv_w, conv_dw_b[l][None, :], conv_ln_g[l][None, :], conv_ln_b[l][None, :], conv_pw2[l].astype(BF16))
    uu0 = jnp.zeros((N_TOK, CONV_CH), BF16)
    uu1 = _conv(u, *conv_args, uu0, seq_len=CTX_LEN, n_seq=N_CTX_SEQ, first_block=0, latent=False)
    uu = _conv(u, *conv_args, uu1, seq_len=LAT_LEN, n_seq=N_LAT_SEQ, first_block=N_CTX_TOK // LAT_LEN, latent=True)

    x1, h2t, st = _mid(mo, uu, xp, xs, mod3, norm2_g[l][None, :], wo1, wo2, wqt, keys)
    pa, pb = _select(st)
    yp, ys = _experts(h2t, pa, pb, u_bf, vt_bf, x1, mod3, final_norm_g[None, :])

    y_prompt = yp.reshape(N_CTX_SEQ, CTX_LEN, D_MODEL)
    y_sample = ys.reshape(N_LAT_SEQ, LAT_LEN, D_MODEL)
    return (y_prompt, y_sample, new_f[:, None], new_b[:, None])
```

```python
import functools

import jax
import jax.numpy as jnp
from jax import lax
from jax.experimental import pallas as pl
from jax.experimental.pallas import tpu as pltpu

F32 = jnp.float32
BF16 = jnp.bfloat16

D_MODEL = 1024
N_CTX_SEQ = 32
CTX_LEN = 256
N_LAT_SEQ = 8
LAT_LEN = 1024
N_CTX_TOK = N_CTX_SEQ * CTX_LEN
N_LAT_TOK = N_LAT_SEQ * LAT_LEN
N_TOK = N_CTX_TOK + N_LAT_TOK
GRID_W = 64
GRID_H = LAT_LEN // GRID_W
GLA_HEADS = 4
GLA_DK = 256
GLA_DV = 512
GLA_HK = 64
GLA_HV = 128
GLA_RANK = 16
GLA_NORMALIZER = 16.0
CHUNK = 64
CUM_ROWS = 256
CONV_CH = 512
CONV_K = 31
CONV_PAD = CONV_K // 2
PEER_HEADS = 8
PEER_NKEYS = 128
PEER_N = PEER_NKEYS * PEER_NKEYS
PEER_DQ = 256
PEER_TOPK = 16
EPS = 1e-6
N_MOD_ROWS = 1 + N_LAT_SEQ
N_MOD_PAD = 16

LANES = 128
SUBLANES = 8
VMEM_LIMIT_BYTES = 56 * 1024 * 1024

TOK_TILE = 512
SEL_TILE = LANES
EXP_TOK_TILE = 512
EXP_TILE = 1024
EXP_SUB = 256
SEL_ROWS_PER_HEAD = 2 * PEER_NKEYS
BF16_ROWS = 16


def _params(*semantics):
    return pltpu.CompilerParams(dimension_semantics=semantics, vmem_limit_bytes=VMEM_LIMIT_BYTES)


def _mod_row(tile, tile_rows):
    first = tile * tile_rows
    return jnp.where(first < N_CTX_TOK, 0, 1 + (first - N_CTX_TOK) // LAT_LEN)


def _ctx_spec(tile_rows):
    n_ctx_tiles = N_CTX_TOK // tile_rows
    return pl.BlockSpec((tile_rows, D_MODEL), lambda i: (jnp.minimum(i, n_ctx_tiles - 1), 0))


def _lat_spec(tile_rows):
    n_ctx_tiles = N_CTX_TOK // tile_rows
    return pl.BlockSpec((tile_rows, D_MODEL), lambda i: (jnp.maximum(i - n_ctx_tiles, 0), 0))


def _token_tile(xp_ref, xs_ref, tile_rows):
    is_ctx = pl.program_id(0) < N_CTX_TOK // tile_rows
    return jnp.where(is_ctx, xp_ref[...], xs_ref[...])


def _split_bf16(a):
    hi = a.astype(BF16)
    lo = (a - hi.astype(F32)).astype(BF16)
    return hi, lo


def _dot(a, b):
    return jnp.dot(a, b, preferred_element_type=F32)


def _dot_nt(a, b):
    return lax.dot_general(a, b, (((1,), (1,)), ((), ())), preferred_element_type=F32)


def _dot_tn(a, b):
    return lax.dot_general(a, b, (((0,), (0,)), ((), ())), preferred_element_type=F32)


def _mod_kernel(c_ref, w_ref, b_ref, o_ref):
    c = c_ref[...]
    a = c * jax.nn.sigmoid(c)
    a_hi, a_lo = _split_bf16(a)
    w_hi, w_lo = _split_bf16(w_ref[...])
    o_ref[...] = _dot(a_hi, w_hi) + _dot(a_hi, w_lo) + _dot(a_lo, w_hi) + b_ref[...]


def _modulation(c_all, w_mod, b_mod):
    n_out = w_mod.shape[1]
    tile = 1024
    return pl.pallas_call(
        _mod_kernel,
        grid=(n_out // tile,),
        in_specs=[
            pl.BlockSpec((N_MOD_PAD, D_MODEL), lambda j: (0, 0)),
            pl.BlockSpec((D_MODEL, tile), lambda j: (0, j)),
            pl.BlockSpec((1, tile), lambda j: (0, j)),
        ],
        out_specs=pl.BlockSpec((N_MOD_PAD, tile), lambda j: (0, j)),
        out_shape=jax.ShapeDtypeStruct((N_MOD_PAD, n_out), F32),
        compiler_params=_params("parallel"),
        name="mod",
    )(c_all, w_mod, b_mod)


N_MAIN = 2 * GLA_DK + 2 * GLA_DV + 2 * CONV_CH


def _mixin_kernel(xp_ref, xs_ref, mod_ref, g1_ref, wm_ref, wab_ref, wup_ref, bup_ref,
                  q_ref, k_ref, v_ref, gf_ref, gb_ref, sg_ref, u_ref):
    x = _token_tile(xp_ref, xs_ref, TOK_TILE)
    mod = mod_ref[0]
    shift1 = mod[:, 0:D_MODEL]
    scale1 = mod[:, D_MODEL:2 * D_MODEL]
    ms = jnp.mean(x * x, axis=-1, keepdims=True)
    h = (x * lax.rsqrt(ms + EPS)) * g1_ref[...]
    h = h * (1.0 + scale1) + shift1
    hb = h.astype(BF16)
    p = _dot(hb, wm_ref[...])
    q_ref[...] = p[:, 0:256] * (GLA_HK ** -0.5)
    k_ref[...] = p[:, 256:512]
    v_ref[...] = p[:, 512:1024]
    g = p[:, 1024:1536]
    sg_ref[...] = g * jax.nn.sigmoid(g)
    ca = p[:, 1536:2048]
    cb = p[:, 2048:2560]
    u_ref[...] = ca * jax.nn.sigmoid(cb)
    a = _dot(hb, wab_ref[...])
    z = _dot(a.astype(BF16), wup_ref[...]) + bup_ref[...]
    ls = jnp.minimum(z, 0.0) - jnp.log(1.0 + jnp.exp(-jnp.abs(z)))
    gl = ls * (1.0 / GLA_NORMALIZER)
    gf_ref[...] = gl[:, 0:GLA_DK]
    gb_ref[...] = gl[:, GLA_DK:2 * GLA_DK]


def _mixin(xp, xs, mod3, g1, w_main, w_ab, w_up, b_up):
    tm = TOK_TILE
    row = lambda i: (i, 0)
    const = lambda i: (0, 0)
    outs = [(GLA_DK, F32), (GLA_DK, F32), (GLA_DV, F32), (GLA_DK, F32), (GLA_DK, F32),
            (GLA_DV, F32), (CONV_CH, F32)]
    return pl.pallas_call(
        _mixin_kernel,
        grid=(N_TOK // tm,),
        in_specs=[
            _ctx_spec(tm), _lat_spec(tm),
            pl.BlockSpec((1, 1, 6 * D_MODEL), lambda i: (_mod_row(i, tm), 0, 0)),
            pl.BlockSpec((1, D_MODEL), const),
            pl.BlockSpec((D_MODEL, N_MAIN), const),
            pl.BlockSpec((D_MODEL, LANES), const),
            pl.BlockSpec((LANES, 2 * GLA_DK), const),
            pl.BlockSpec((1, 2 * GLA_DK), const),
        ],
        out_specs=[pl.BlockSpec((tm, n), row) for n, _ in outs],
        out_shape=[jax.ShapeDtypeStruct((N_TOK, n), dt) for n, dt in outs],
        compiler_params=_params("parallel"),
        name="mixin",
    )(xp, xs, mod3, g1, w_main, w_ab, w_up, b_up)


def _gla_kernel(*refs, seq_len, has_init, emit_state):
    q_ref, k_ref, v_ref, gf_ref, gb_ref, sg_ref, gn_ref = refs[:7]
    pos = 7
    if has_init:
        s0f_ref, s0b_ref = refs[pos:pos + 2]
        pos += 2
    pos += 1
    o_ref = refs[pos]
    pos += 1
    if emit_state:
        sf_ref, sb_ref = refs[pos:pos + 2]
        pos += 2
    oacc_ref, stf_ref, stb_ref, bf_ref, bb_ref = refs[pos:pos + 5]

    n_chunks = seq_len // CHUNK
    lane = lax.broadcasted_iota(jnp.int32, (1, GLA_DK), 1)
    head_mask = [(lane >= h * GLA_HK) & (lane < (h + 1) * GLA_HK) for h in range(GLA_HEADS)]
    srow = lax.broadcasted_iota(jnp.int32, (GLA_HEADS * CHUNK, CHUNK), 0) % CHUNK
    scol = lax.broadcasted_iota(jnp.int32, (GLA_HEADS * CHUNK, CHUNK), 1)
    ones_t = jnp.ones((CHUNK, GLA_HV), BF16)

    gi = lax.broadcasted_iota(jnp.int32, (CUM_ROWS, CUM_ROWS), 0)
    gj = lax.broadcasted_iota(jnp.int32, (CUM_ROWS, CUM_ROWS), 1)
    same_chunk = (gi // CHUNK) == (gj // CHUNK)
    tri_f = (same_chunk & (gi >= gj)).astype(BF16)
    tri_b = (same_chunk & (gi <= gj)).astype(BF16)
    for r in range(seq_len // CUM_ROWS):
        grp = slice(r * CUM_ROWS, (r + 1) * CUM_ROWS)
        for g_ref, b_ref, tri in ((gf_ref, bf_ref, tri_f), (gb_ref, bb_ref, tri_b)):
            g_hi, g_lo = _split_bf16(g_ref[grp, :])
            b_ref[grp, :] = _dot(tri, g_hi) + _dot(tri, g_lo)

    def chunk(c, g_ref, b_ref, st_ref, forward):
        if forward:
            causal = srow >= scol
            ref_row, last_row = CHUNK // 2 - 1, CHUNK - 1
        else:
            causal = srow <= scol
            ref_row, last_row = CHUNK // 2, 0
        r0 = pl.multiple_of(c * CHUNK, CHUNK)
        rows = pl.ds(r0, CHUNK)
        g = g_ref[rows, :]
        q = q_ref[rows, :]
        k = k_ref[rows, :]
        vb = v_ref[rows, :].astype(BF16)
        g_hi, g_lo = _split_bf16(g)
        b = b_ref[rows, :]
        b_mid = b[ref_row:ref_row + 1, :]
        b_end = b[last_row:last_row + 1, :]
        qt = q * jnp.exp(b - b_mid)
        kt = (k * jnp.exp(b_mid - b)).astype(BF16)
        qe = q * jnp.exp(b)
        kd = (k * jnp.exp(b_end - b)).astype(BF16)
        q_stack = jnp.concatenate([jnp.where(m, qt, 0.0) for m in head_mask], axis=0).astype(BF16)
        qe_stack = jnp.concatenate([jnp.where(m, qe, 0.0) for m in head_mask], axis=0).astype(BF16)
        scores = jnp.where(causal, _dot_nt(q_stack, kt), 0.0)
        o_intra = _dot(scores.astype(BF16), vb)
        state = st_ref[...]
        o_inter = _dot(qe_stack, state.astype(BF16))
        upd = _dot_tn(kd, vb)
        dec = jnp.exp(_dot_tn(g_hi, ones_t) + _dot_tn(g_lo, ones_t))
        for h in range(GLA_HEADS):
            hr = slice(h * CHUNK, (h + 1) * CHUNK)
            hv = slice(h * GLA_HV, (h + 1) * GLA_HV)
            oacc_ref[rows, hv] += o_intra[hr, hv] + o_inter[hr, :]
            st_ref[hr, :] = dec[hr, :] * state[hr, :] + upd[hr, hv]

    oacc_ref[...] = jnp.zeros_like(oacc_ref)
    if has_init:
        stf_ref[...] = s0f_ref[0].reshape(GLA_HEADS * GLA_HK, GLA_HV)
        stb_ref[...] = s0b_ref[0].reshape(GLA_HEADS * GLA_HK, GLA_HV)
    else:
        stf_ref[...] = jnp.zeros_like(stf_ref)
        stb_ref[...] = jnp.zeros_like(stb_ref)

    def step(it, carry):
        chunk(it, gf_ref, bf_ref, stf_ref, True)
        chunk(n_chunks - 1 - it, gb_ref, bb_ref, stb_ref, False)
        return carry

    lax.fori_loop(0, n_chunks, step, 0)
    if emit_state:
        sf_ref[0] = stf_ref[...].reshape(GLA_HEADS, GLA_HK, GLA_HV)
        sb_ref[0] = stb_ref[...].reshape(GLA_HEADS, GLA_HK, GLA_HV)

    o = oacc_ref[...]
    gn = gn_ref[...]
    sg = sg_ref[...]
    for h in range(GLA_HEADS):
        hv = slice(h * GLA_HV, (h + 1) * GLA_HV)
        oh = o[:, hv]
        ms = jnp.mean(oh * oh, axis=-1, keepdims=True)
        o_ref[:, hv] = ((oh * lax.rsqrt(ms + EPS)) * gn[:, hv] * sg[:, hv]).astype(o_ref.dtype)


def _gla(q, k, v, gf, gb, sg, gn, prev_out, *, seq_len, n_seq, first_block, init_states, emit_state):
    row = lambda i: (first_block + i, 0)
    const = lambda i: (0, 0)
    st_spec = pl.BlockSpec((1, GLA_HEADS, GLA_HK, GLA_HV), lambda i: (i, 0, 0, 0))
    in_specs = [
        pl.BlockSpec((seq_len, GLA_DK), row), pl.BlockSpec((seq_len, GLA_DK), row),
        pl.BlockSpec((seq_len, GLA_DV), row), pl.BlockSpec((seq_len, GLA_DK), row),
        pl.BlockSpec((seq_len, GLA_DK), row), pl.BlockSpec((seq_len, GLA_DV), row),
        pl.BlockSpec((1, GLA_DV), const),
    ]
    args = [q, k, v, gf, gb, sg, gn]
    if init_states is not None:
        in_specs += [st_spec, st_spec]
        args += list(init_states)
    in_specs.append(pl.BlockSpec(memory_space=pl.ANY))
    args.append(prev_out)
    out_specs = [pl.BlockSpec((seq_len, GLA_DV), row)]
    out_shape = [jax.ShapeDtypeStruct(prev_out.shape, prev_out.dtype)]
    if emit_state:
        out_specs += [st_spec, st_spec]
        out_shape += [jax.ShapeDtypeStruct((n_seq, GLA_HEADS, GLA_HK, GLA_HV), F32)] * 2
    return pl.pallas_call(
        functools.partial(_gla_kernel, seq_len=seq_len, has_init=init_states is not None, emit_state=emit_state),
        grid=(n_seq,),
        in_specs=in_specs,
        out_specs=out_specs,
        out_shape=out_shape,
        scratch_shapes=[pltpu.VMEM((seq_len, GLA_DV), F32), pltpu.VMEM((GLA_HEADS * GLA_HK, GLA_HV), F32),
                        pltpu.VMEM((GLA_HEADS * GLA_HK, GLA_HV), F32),
                        pltpu.VMEM((seq_len, GLA_DK), F32), pltpu.VMEM((seq_len, GLA_DK), F32)],
        input_output_aliases={len(args) - 1: 0},
        compiler_params=_params("parallel"),
        name="gla_lat" if init_states is not None else "gla_ctx",
    )(*args)


ROW_PITCH = GRID_W + 2 * (CONV_PAD + 1)


def _conv_kernel(u_ref, w_ref, b_ref, lng_ref, lnb_ref, pw_ref, prev_ref, o_ref, pad_ref, y_ref, *, latent, seq_len):
    del prev_ref
    half = CONV_CH // 2
    blk = GRID_W
    n_blk = seq_len // blk
    n_pad = pad_ref.shape[1]
    pad_ref[0] = jnp.zeros(pad_ref.shape[1:], F32)
    if latent:
        for r in range(n_blk):
            pad_ref[0, r * ROW_PITCH + CONV_PAD + 1:r * ROW_PITCH + CONV_PAD + 1 + blk, :] = u_ref[r * blk:(r + 1) * blk, 0:half]
    else:
        pad_ref[0, CONV_PAD + 1:CONV_PAD + 1 + seq_len, :] = u_ref[...]
    for s in range(1, SUBLANES):
        pad_ref[s, 0:n_pad - SUBLANES, :] = pad_ref[0, s:n_pad - SUBLANES + s, :]

    def window(first_row, tap):
        q, s = divmod(tap + 1, SUBLANES)
        return pad_ref[s, first_row + q * SUBLANES:first_row + q * SUBLANES + blk, :]

    bias = b_ref[...]
    lng = lng_ref[...]
    lnb = lnb_ref[...]
    for r in range(n_blk):
        if latent:
            acc_w = jnp.zeros((blk, half), F32)
            for t in range(CONV_K):
                acc_w += w_ref[t:t + 1, 0:half] * window(r * ROW_PITCH, t)
            acc_h = jnp.zeros((blk, half), F32)
            for r2 in range(n_blk):
                t = r2 - r + CONV_PAD
                acc_h += w_ref[t:t + 1, half:CONV_CH] * u_ref[r2 * blk:(r2 + 1) * blk, half:CONV_CH]
            y = jnp.concatenate([acc_w, acc_h], axis=-1) + bias
        else:
            acc = jnp.zeros((blk, CONV_CH), F32)
            for t in range(CONV_K):
                acc += w_ref[t:t + 1, :] * window(r * blk, t)
            y = acc + bias
        mu = jnp.mean(y, axis=-1, keepdims=True)
        yc = y - mu
        var = jnp.mean(yc * yc, axis=-1, keepdims=True)
        yn = (yc * lax.rsqrt(var + EPS)) * lng + lnb
        y_ref[r * blk:(r + 1) * blk, :] = (yn * jax.nn.sigmoid(yn)).astype(BF16)
    o_ref[...] = _dot(y_ref[...], pw_ref[...]).astype(o_ref.dtype)


def _conv(u, w, b, lng, lnb, pw, prev_out, *, seq_len, n_seq, first_block, latent):
    row = lambda i: (first_block + i, 0)
    const = lambda i: (0, 0)
    if latent:
        pad_shape = (SUBLANES, GRID_H * ROW_PITCH, CONV_CH // 2)
    else:
        pad_shape = (SUBLANES, seq_len + 2 * (CONV_PAD + 1), CONV_CH)
    return pl.pallas_call(
        functools.partial(_conv_kernel, latent=latent, seq_len=seq_len),
        grid=(n_seq,),
        in_specs=[
            pl.BlockSpec((seq_len, CONV_CH), row),
            pl.BlockSpec((CONV_K + 1, CONV_CH), const),
            pl.BlockSpec((1, CONV_CH), const), pl.BlockSpec((1, CONV_CH), const), pl.BlockSpec((1, CONV_CH), const),
            pl.BlockSpec((CONV_CH, CONV_CH), const),
            pl.BlockSpec(memory_space=pl.ANY),
        ],
        out_specs=pl.BlockSpec((seq_len, CONV_CH), row),
        out_shape=jax.ShapeDtypeStruct(prev_out.shape, prev_out.dtype),
        scratch_shapes=[pltpu.VMEM(pad_shape, F32), pltpu.VMEM((seq_len, CONV_CH), BF16)],
        input_output_aliases={6: 0},
        compiler_params=_params("parallel"),
        name="conv_lat" if latent else "conv_ctx",
    )(u, w, b, lng, lnb, pw, prev_out)


def _mid_kernel(mo_ref, uu_ref, xp_ref, xs_ref, mod_ref, g2_ref, wo1_ref, wo2_ref, wqt_ref, keys_ref,
                x1_ref, h2t_ref, st_ref):
    mod = mod_ref[0]
    gate1 = mod[:, 2 * D_MODEL:3 * D_MODEL]
    shift2 = mod[:, 3 * D_MODEL:4 * D_MODEL]
    scale2 = mod[:, 4 * D_MODEL:5 * D_MODEL]
    m = _dot(mo_ref[...], wo1_ref[...]) + _dot(uu_ref[...], wo2_ref[...])
    x1 = _token_tile(xp_ref, xs_ref, TOK_TILE) + gate1 * m
    x1_ref[...] = x1
    ms = jnp.mean(x1 * x1, axis=-1, keepdims=True)
    h2 = (x1 * lax.rsqrt(ms + EPS)) * g2_ref[...]
    h2 = h2 * (1.0 + scale2) + shift2
    h2t = h2.T.astype(BF16)
    h2t_ref[...] = h2t
    qt = _dot(wqt_ref[...], h2t)
    for hp in range(2 * PEER_HEADS):
        rows = slice(hp * PEER_NKEYS, (hp + 1) * PEER_NKEYS)
        st_ref[rows, :] = _dot(keys_ref[hp], qt[rows, :].astype(BF16))


def _mid(mo, uu, xp, xs, mod3, g2, wo1, wo2, wqt, keys):
    tm = TOK_TILE
    row = lambda i: (i, 0)
    col = lambda i: (0, i)
    const = lambda i: (0, 0)
    n_q = PEER_HEADS * PEER_DQ
    return pl.pallas_call(
        _mid_kernel,
        grid=(N_TOK // tm,),
        in_specs=[
            pl.BlockSpec((tm, GLA_DV), row), pl.BlockSpec((tm, CONV_CH), row), _ctx_spec(tm), _lat_spec(tm),
            pl.BlockSpec((1, 1, 6 * D_MODEL), lambda i: (_mod_row(i, tm), 0, 0)),
            pl.BlockSpec((1, D_MODEL), const),
            pl.BlockSpec((GLA_DV, D_MODEL), const), pl.BlockSpec((CONV_CH, D_MODEL), const),
            pl.BlockSpec((n_q, D_MODEL), const),
            pl.BlockSpec((2 * PEER_HEADS, PEER_NKEYS, PEER_DQ // 2), lambda i: (0, 0, 0)),
        ],
        out_specs=[pl.BlockSpec((tm, D_MODEL), row), pl.BlockSpec((D_MODEL, tm), col), pl.BlockSpec((n_q, tm), col)],
        out_shape=[jax.ShapeDtypeStruct((N_TOK, D_MODEL), F32), jax.ShapeDtypeStruct((D_MODEL, N_TOK), BF16),
                   jax.ShapeDtypeStruct((n_q, N_TOK), F32)],
        compiler_params=_params("parallel"),
        name="mid",
    )(mo, uu, xp, xs, mod3, g2, wo1, wo2, wqt, keys)


N_CAND_ROWS = 80


def _top16(s, key_iota, sv_ref, exact, want_rank):
    rank = jnp.full(s.shape, PEER_TOPK, jnp.int32) if want_rank else None
    for a in range(PEER_TOPK):
        m = jnp.max(s, axis=0, keepdims=True)
        hit = s == m
        if exact:
            first = jnp.min(jnp.where(hit, key_iota, PEER_NKEYS), axis=0, keepdims=True)
            hit = key_iota == first
        if want_rank:
            rank = jnp.where(hit, a, rank)
        s = jnp.where(hit, -jnp.inf, s)
        sv_ref[a:a + 1, :] = m
    return rank, s


def _count_true(mask):
    return jnp.sum(jnp.where(mask, 1.0, 0.0), axis=0, keepdims=True)


def _select_head(h, bad, st_ref, pa_ref, pb_ref, sv1_ref, sv2_ref, cand_ref, *, exact):
    key_iota = lax.broadcasted_iota(jnp.int32, (PEER_NKEYS, SEL_TILE), 0)
    crow = lax.broadcasted_iota(jnp.int32, (N_CAND_ROWS, SEL_TILE), 0)
    flat = jnp.where(crow < 16, crow * 16,
                     jnp.where(crow < 72, ((crow - 16) % 8) * 16 + (crow - 16) // 8 + 1, crow - 72 + 8))
    base = pl.multiple_of(h * 2 * PEER_NKEYS, 2 * PEER_NKEYS)
    s1 = st_ref[pl.ds(base, PEER_NKEYS), :]
    s2 = st_ref[pl.ds(base + PEER_NKEYS, PEER_NKEYS), :]
    r1, _ = _top16(s1, key_iota, sv1_ref, exact, True)
    r2, s2_left = _top16(s2, key_iota, sv2_ref, exact, exact)
    top1 = sv1_ref[0:1, :]
    top2 = sv2_ref[0:1, :]
    cand_ref[0:16, :] = sv1_ref[...] + top2
    for b in range(1, 8):
        cand_ref[8 + 8 * b:16 + 8 * b, :] = sv1_ref[0:8, :] + sv2_ref[b:b + 1, :]
    cand_ref[72:80, :] = top1 + sv2_ref[8:16, :]
    cand = cand_ref[...]
    cmax = top1 + top2
    sel = jnp.zeros(cand.shape, jnp.bool_)
    z = jnp.zeros((1, SEL_TILE), F32)
    for _ in range(PEER_TOPK):
        m = jnp.max(cand, axis=0, keepdims=True)
        hit = cand == m
        if exact:
            first = jnp.min(jnp.where(hit, flat, 256), axis=0, keepdims=True)
            hit = flat == first
        sel = sel | hit
        cand = jnp.where(hit, -jnp.inf, cand)
        z = z + jnp.exp(m - cmax)
    self32 = jnp.where(sel, 1.0, 0.0)
    counts = [jnp.sum(self32[0:16, :], axis=0, keepdims=True)]
    for b in range(1, 8):
        counts.append(jnp.sum(self32[8 + 8 * b:16 + 8 * b, :], axis=0, keepdims=True))
    for b in range(8, 16):
        counts.append(self32[72 + b - 8:72 + b - 7, :])
    n2 = jnp.zeros((PEER_NKEYS, SEL_TILE), F32)
    for b in range(PEER_TOPK):
        is_b = (r2 == b) if exact else (s2 == sv2_ref[b:b + 1, :])
        n2 = jnp.where(is_b, counts[b], n2)
    out = pl.multiple_of(h * SEL_ROWS_PER_HEAD, SEL_ROWS_PER_HEAD)
    pa_ref[pl.ds(out, PEER_NKEYS), :] = r1.astype(F32)
    pa_ref[pl.ds(out + PEER_NKEYS, PEER_NKEYS), :] = jnp.exp(s1 - top1)
    pb_ref[pl.ds(out, PEER_NKEYS), :] = n2.astype(BF16)
    pb_ref[pl.ds(out + PEER_NKEYS, PEER_NKEYS), :] = (jnp.exp(s2 - top2) / z).astype(BF16)
    if exact:
        return bad
    n_taken = _count_true(r1 < PEER_TOPK) + _count_true(s2_left == -jnp.inf) + jnp.sum(self32, axis=0, keepdims=True)
    return jnp.maximum(bad, jnp.where(n_taken == 3.0 * PEER_TOPK, 0.0, 1.0))


HEADS_PER_TRIP = 2


def _select_kernel(st_ref, pa_ref, pb_ref, sv1_ref, sv2_ref, cand_ref):
    def trip(exact):
        def body(t, bad):
            for k in range(HEADS_PER_TRIP):
                bad = _select_head(t * HEADS_PER_TRIP + k, bad, st_ref, pa_ref, pb_ref,
                                   sv1_ref.at[k], sv2_ref.at[k], cand_ref.at[k], exact=exact)
            return bad
        return body

    bad = lax.fori_loop(0, PEER_HEADS // HEADS_PER_TRIP, trip(False), jnp.zeros((1, SEL_TILE), F32))

    @pl.when(jnp.max(bad) > 0.0)
    def _():
        lax.fori_loop(0, PEER_HEADS // HEADS_PER_TRIP, trip(True), bad)


def _select(st):
    n_q = PEER_HEADS * PEER_DQ
    n_pe = PEER_HEADS * SEL_ROWS_PER_HEAD
    col = lambda i: (0, i)
    return pl.pallas_call(
        _select_kernel,
        grid=(N_TOK // SEL_TILE,),
        in_specs=[pl.BlockSpec((n_q, SEL_TILE), col)],
        out_specs=[pl.BlockSpec((n_pe, SEL_TILE), col), pl.BlockSpec((n_pe, SEL_TILE), col)],
        out_shape=[jax.ShapeDtypeStruct((n_pe, N_TOK), F32), jax.ShapeDtypeStruct((n_pe, N_TOK), BF16)],
        scratch_shapes=[pltpu.VMEM((HEADS_PER_TRIP, PEER_TOPK, SEL_TILE), F32),
                        pltpu.VMEM((HEADS_PER_TRIP, PEER_TOPK, SEL_TILE), F32),
                        pltpu.VMEM((HEADS_PER_TRIP, N_CAND_ROWS, SEL_TILE), F32)],
        compiler_params=_params("parallel"),
        name="select",
    )(st)


N_EXP_TILES = PEER_N // EXP_TILE
N_EXP_TOK_TILES = N_TOK // EXP_TOK_TILE
N_EXP_WORK = N_EXP_TOK_TILES * N_EXP_TILES
N_EXP_STEPS = N_EXP_WORK + 2


def _experts_kernel(h2t_ref, pa_ref, pb_ref, u_ref, vt_ref, x1_ref, mod_ref, gfin_ref, yp_ref, ys_ref,
                    acc_ref, a_ref, c_ref):
    s = pl.program_id(0)
    item_gate = jnp.clip(s - 1, 0, N_EXP_WORK - 1)
    e_gate = item_gate % N_EXP_TILES
    e_out = jnp.maximum(s - 2, 0) % N_EXP_TILES
    n_lane_chunks = EXP_TOK_TILE // LANES
    tile3 = (PEER_NKEYS // BF16_ROWS, BF16_ROWS, LANES)

    @pl.when(s == 0)
    def _():
        a_ref[...] = jnp.zeros_like(a_ref)
        c_ref[...] = jnp.zeros_like(c_ref)

    @pl.when((s == 0) | ((s >= 2) & (e_out == 0)))
    def _():
        acc_ref[...] = jnp.zeros_like(acc_ref)

    acc_ref[...] += _dot(vt_ref[...], c_ref[...])

    for ib in range(EXP_TILE // PEER_NKEYS):
        i = e_gate * (EXP_TILE // PEER_NKEYS) + ib
        rows = slice(ib * PEER_NKEYS, (ib + 1) * PEER_NKEYS)
        rank_rows = [pa_ref[pl.ds(h * SEL_ROWS_PER_HEAD + i, 1), :] for h in range(PEER_HEADS)]
        e1_rows = [pa_ref[pl.ds(h * SEL_ROWS_PER_HEAD + PEER_NKEYS + i, 1), :] for h in range(PEER_HEADS)]
        for lc in range(n_lane_chunks):
            lanes = slice(lc * LANES, (lc + 1) * LANES)
            a = a_ref[rows, lanes]
            act = (0.5 * a * (1.0 + lax.erf(a * (2.0 ** -0.5)))).astype(BF16).reshape(tile3)
            w = jnp.zeros(tile3, BF16)
            for h in range(PEER_HEADS):
                base = h * SEL_ROWS_PER_HEAD
                rank_i = jnp.broadcast_to(rank_rows[h][:, lanes], (BF16_ROWS, LANES)).astype(BF16)
                e1_i = jnp.broadcast_to(e1_rows[h][:, lanes], (BF16_ROWS, LANES)).astype(BF16)
                n2 = pb_ref[base:base + PEER_NKEYS, lanes].reshape(tile3)
                e2 = pb_ref[base + PEER_NKEYS:base + 2 * PEER_NKEYS, lanes].reshape(tile3)
                w = w + jnp.where(rank_i[None] < n2, e1_i[None] * e2, jnp.zeros((), BF16))
            c_ref[rows, lanes] = (act * w).reshape(PEER_NKEYS, LANES)

    a_ref[...] = _dot(u_ref[...], h2t_ref[...])

    @pl.when((s >= 2) & (e_out == N_EXP_TILES - 1))
    def _():
        mod = mod_ref[0]
        gate2 = mod[:, 5 * D_MODEL:6 * D_MODEL]
        y = x1_ref[...] + gate2 * acc_ref[...].T
        ms = jnp.mean(y * y, axis=-1, keepdims=True)
        y = (y * lax.rsqrt(ms + EPS)) * gfin_ref[...]
        is_ctx = jnp.maximum(s - 2, 0) // N_EXP_TILES < N_CTX_TOK // EXP_TOK_TILE

        @pl.when(is_ctx)
        def _():
            yp_ref[...] = y

        @pl.when(jnp.logical_not(is_ctx))
        def _():
            ys_ref[...] = y


def _experts(h2t, pa, pb, u_bf, vt_bf, x1, mod3, gfin):
    tm = EXP_TOK_TILE
    n_pe = PEER_HEADS * SEL_ROWS_PER_HEAD
    item = lambda s, lag: jnp.clip(s - lag, 0, N_EXP_WORK - 1)
    tok = lambda s, lag: item(s, lag) // N_EXP_TILES
    exp = lambda s, lag: item(s, lag) % N_EXP_TILES
    n_ctx_tiles = N_CTX_TOK // tm
    return pl.pallas_call(
        _experts_kernel,
        grid=(N_EXP_STEPS,),
        in_specs=[
            pl.BlockSpec((D_MODEL, tm), lambda s: (0, tok(s, 0))),
            pl.BlockSpec((n_pe, tm), lambda s: (0, tok(s, 1))),
            pl.BlockSpec((n_pe, tm), lambda s: (0, tok(s, 1))),
            pl.BlockSpec((EXP_TILE, D_MODEL), lambda s: (exp(s, 0), 0)),
            pl.BlockSpec((D_MODEL, EXP_TILE), lambda s: (0, exp(s, 2))),
            pl.BlockSpec((tm, D_MODEL), lambda s: (tok(s, 2), 0)),
            pl.BlockSpec((1, 1, 6 * D_MODEL), lambda s: (_mod_row(tok(s, 2), tm), 0, 0)),
            pl.BlockSpec((1, D_MODEL), lambda s: (0, 0)),
        ],
        out_specs=[pl.BlockSpec((tm, D_MODEL), lambda s: (jnp.minimum(tok(s, 2), n_ctx_tiles - 1), 0)),
                   pl.BlockSpec((tm, D_MODEL), lambda s: (jnp.maximum(tok(s, 2) - n_ctx_tiles, 0), 0))],
        out_shape=[jax.ShapeDtypeStruct((N_CTX_TOK, D_MODEL), F32), jax.ShapeDtypeStruct((N_LAT_TOK, D_MODEL), F32)],
        scratch_shapes=[pltpu.VMEM((D_MODEL, tm), F32), pltpu.VMEM((EXP_TILE, tm), F32),
                        pltpu.VMEM((EXP_TILE, tm), BF16)],
        compiler_params=_params("arbitrary"),
        name="experts",
    )(h2t, pa, pb, u_bf, vt_bf, x1, mod3, gfin)


def kernel(x_prompt, x_sample, c, state_gla_fwd, state_gla_bwd, c_ctx, norm1_g, w_mod, b_mod, w_in, w_af_up, b_af,
           w_ab_up, b_ab, gla_norm_g, conv_dw_w, conv_dw_b, conv_ln_g, conv_ln_b, conv_pw2, w_out, norm2_g, peer_wq,
           peer_subkeys, peer_u, peer_v, final_norm_g):
    depth = w_in.shape[0]
    assert depth == 1
    l = 0
    xp = x_prompt.reshape(N_CTX_TOK, D_MODEL)
    xs = x_sample.reshape(N_LAT_TOK, D_MODEL)

    c_all = jnp.zeros((N_MOD_PAD, D_MODEL), F32).at[0].set(c_ctx).at[1:N_MOD_ROWS].set(c)
    w = w_in[l]
    w_main = jnp.concatenate([w[:, 0:1536], w[:, 1568:2592]], axis=1).astype(BF16)
    w_ab = jnp.zeros((D_MODEL, LANES), F32).at[:, 0:2 * GLA_RANK].set(w[:, 1536:1568]).astype(BF16)
    w_up = jnp.zeros((LANES, 2 * GLA_DK), F32)
    w_up = w_up.at[0:GLA_RANK, 0:GLA_DK].set(w_af_up[l]).at[GLA_RANK:2 * GLA_RANK, GLA_DK:].set(w_ab_up[l]).astype(BF16)
    b_up = jnp.concatenate([b_af[l], b_ab[l]])[None, :]
    conv_w = jnp.zeros((CONV_K + 1, CONV_CH), F32).at[0:CONV_K].set(conv_dw_w[l])
    wo1 = w_out[l][0:GLA_DV].astype(BF16)
    wo2 = w_out[l][GLA_DV:].astype(BF16)
    wqt = peer_wq[l].T.astype(BF16)
    keys = peer_subkeys[l].reshape(2 * PEER_HEADS, PEER_NKEYS, PEER_DQ // 2).astype(BF16)
    u_bf = peer_u[l].astype(BF16)
    vt_bf = peer_v[l].T.astype(BF16)

    mod = _modulation(c_all, w_mod[l], b_mod[l][None, :])
    mod3 = mod[:, None, :]

    q, k, v, gf, gb, sg, u = _mixin(xp, xs, mod3, norm1_g[l][None, :], w_main, w_ab, w_up, b_up)

    gn = gla_norm_g[l][None, :]
    mo0 = jnp.zeros((N_TOK, GLA_DV), BF16)
    mo1, new_f, new_b = _gla(q, k, v, gf, gb, sg, gn, mo0, seq_len=CTX_LEN, n_seq=N_CTX_SEQ, first_block=0,
                             init_states=None, emit_state=True)
    (mo,) = _gla(q, k, v, gf, gb, sg, gn, mo1, seq_len=LAT_LEN, n_seq=N_LAT_SEQ,
                 first_block=N_CTX_TOK // LAT_LEN, init_states=(state_gla_fwd[:, l], state_gla_bwd[:, l]),
                 emit_state=False)

    conv_args = (conv_w, conv_dw_b[l][None, :], conv_ln_g[l][None, :], conv_ln_b[l][None, :], conv_pw2[l].astype(BF16))
    uu0 = jnp.zeros((N_TOK, CONV_CH), BF16)
    uu1 = _conv(u, *conv_args, uu0, seq_len=CTX_LEN, n_seq=N_CTX_SEQ, first_block=0, latent=False)
    uu = _conv(u, *conv_args, uu1, seq_len=LAT_LEN, n_seq=N_LAT_SEQ, first_block=N_CTX_TOK // LAT_LEN, latent=True)

    x1, h2t, st = _mid(mo, uu, xp, xs, mod3, norm2_g[l][None, :], wo1, wo2, wqt, keys)
    pa, pb = _select(st)
    yp, ys = _experts(h2t, pa, pb, u_bf, vt_bf, x1, mod3, final_norm_g[None, :])

    y_prompt = yp.reshape(N_CTX_SEQ, CTX_LEN, D_MODEL)
    y_sample = ys.reshape(N_LAT_SEQ, LAT_LEN, D_MODEL)
    return (y_prompt, y_sample, new_f[:, None], new_b[:, None])
```

```python
import functools

import jax
import jax.numpy as jnp
from jax import lax
from jax.experimental import pallas as pl
from jax.experimental.pallas import tpu as pltpu

F32 = jnp.float32
BF16 = jnp.bfloat16

D_MODEL = 1024
N_CTX_SEQ = 32
CTX_LEN = 256
N_LAT_SEQ = 8
LAT_LEN = 1024
N_CTX_TOK = N_CTX_SEQ * CTX_LEN
N_LAT_TOK = N_LAT_SEQ * LAT_LEN
N_TOK = N_CTX_TOK + N_LAT_TOK
GRID_W = 64
GRID_H = LAT_LEN // GRID_W
GLA_HEADS = 4
GLA_DK = 256
GLA_DV = 512
GLA_HK = 64
GLA_HV = 128
GLA_RANK = 16
GLA_NORMALIZER = 16.0
CHUNK = 64
CUM_ROWS = 256
CONV_CH = 512
CONV_K = 31
CONV_PAD = CONV_K // 2
PEER_HEADS = 8
PEER_NKEYS = 128
PEER_N = PEER_NKEYS * PEER_NKEYS
PEER_DQ = 256
PEER_TOPK = 16
EPS = 1e-6
N_MOD_ROWS = 1 + N_LAT_SEQ
N_MOD_PAD = 16

LANES = 128
SUBLANES = 8
VMEM_LIMIT_BYTES = 56 * 1024 * 1024

TOK_TILE = 512
SEL_TILE = LANES
EXP_TOK_TILE = 512
EXP_TOK_GROUP = 256
EXP_TILE = 1024
EXP_SUB = 256
SEL_ROWS_PER_HEAD = 2 * PEER_NKEYS
BF16_ROWS = 16


def _params(*semantics):
    return pltpu.CompilerParams(dimension_semantics=semantics, vmem_limit_bytes=VMEM_LIMIT_BYTES)


def _mod_row(tile, tile_rows):
    first = tile * tile_rows
    return jnp.where(first < N_CTX_TOK, 0, 1 + (first - N_CTX_TOK) // LAT_LEN)


def _ctx_spec(tile_rows):
    n_ctx_tiles = N_CTX_TOK // tile_rows
    return pl.BlockSpec((tile_rows, D_MODEL), lambda i: (jnp.minimum(i, n_ctx_tiles - 1), 0))


def _lat_spec(tile_rows):
    n_ctx_tiles = N_CTX_TOK // tile_rows
    return pl.BlockSpec((tile_rows, D_MODEL), lambda i: (jnp.maximum(i - n_ctx_tiles, 0), 0))


def _token_tile(xp_ref, xs_ref, tile_rows):
    is_ctx = pl.program_id(0) < N_CTX_TOK // tile_rows
    return jnp.where(is_ctx, xp_ref[...], xs_ref[...])


def _split_bf16(a):
    hi = a.astype(BF16)
    lo = (a - hi.astype(F32)).astype(BF16)
    return hi, lo


def _dot(a, b):
    return jnp.dot(a, b, preferred_element_type=F32)


def _dot_nt(a, b):
    return lax.dot_general(a, b, (((1,), (1,)), ((), ())), preferred_element_type=F32)


def _dot_tn(a, b):
    return lax.dot_general(a, b, (((0,), (0,)), ((), ())), preferred_element_type=F32)


def _mod_kernel(c_ref, w_ref, b_ref, o_ref):
    c = c_ref[...]
    a = c * jax.nn.sigmoid(c)
    a_hi, a_lo = _split_bf16(a)
    w_hi, w_lo = _split_bf16(w_ref[...])
    o_ref[...] = _dot(a_hi, w_hi) + _dot(a_hi, w_lo) + _dot(a_lo, w_hi) + b_ref[...]


def _modulation(c_all, w_mod, b_mod):
    n_out = w_mod.shape[1]
    tile = 1024
    return pl.pallas_call(
        _mod_kernel,
        grid=(n_out // tile,),
        in_specs=[
            pl.BlockSpec((N_MOD_PAD, D_MODEL), lambda j: (0, 0)),
            pl.BlockSpec((D_MODEL, tile), lambda j: (0, j)),
            pl.BlockSpec((1, tile), lambda j: (0, j)),
        ],
        out_specs=pl.BlockSpec((N_MOD_PAD, tile), lambda j: (0, j)),
        out_shape=jax.ShapeDtypeStruct((N_MOD_PAD, n_out), F32),
        compiler_params=_params("parallel"),
        name="mod",
    )(c_all, w_mod, b_mod)


N_MAIN = 2 * GLA_DK + 2 * GLA_DV + 2 * CONV_CH


def _mixin_kernel(xp_ref, xs_ref, mod_ref, g1_ref, wm_ref, wab_ref, wup_ref, bup_ref,
                  q_ref, k_ref, v_ref, gf_ref, gb_ref, sg_ref, u_ref):
    x = _token_tile(xp_ref, xs_ref, TOK_TILE)
    mod = mod_ref[0]
    shift1 = mod[:, 0:D_MODEL]
    scale1 = mod[:, D_MODEL:2 * D_MODEL]
    ms = jnp.mean(x * x, axis=-1, keepdims=True)
    h = (x * lax.rsqrt(ms + EPS)) * g1_ref[...]
    h = h * (1.0 + scale1) + shift1
    hb = h.astype(BF16)
    p = _dot(hb, wm_ref[...])
    q_ref[...] = p[:, 0:256] * (GLA_HK ** -0.5)
    k_ref[...] = p[:, 256:512]
    v_ref[...] = p[:, 512:1024]
    g = p[:, 1024:1536]
    sg_ref[...] = g * jax.nn.sigmoid(g)
    ca = p[:, 1536:2048]
    cb = p[:, 2048:2560]
    u_ref[...] = ca * jax.nn.sigmoid(cb)
    a = _dot(hb, wab_ref[...])
    z = _dot(a.astype(BF16), wup_ref[...]) + bup_ref[...]
    ls = jnp.minimum(z, 0.0) - jnp.log(1.0 + jnp.exp(-jnp.abs(z)))
    gl = ls * (1.0 / GLA_NORMALIZER)
    gf_ref[...] = gl[:, 0:GLA_DK]
    gb_ref[...] = gl[:, GLA_DK:2 * GLA_DK]


def _mixin(xp, xs, mod3, g1, w_main, w_ab, w_up, b_up):
    tm = TOK_TILE
    row = lambda i: (i, 0)
    const = lambda i: (0, 0)
    outs = [(GLA_DK, F32), (GLA_DK, F32), (GLA_DV, F32), (GLA_DK, F32), (GLA_DK, F32),
            (GLA_DV, F32), (CONV_CH, F32)]
    return pl.pallas_call(
        _mixin_kernel,
        grid=(N_TOK // tm,),
        in_specs=[
            _ctx_spec(tm), _lat_spec(tm),
            pl.BlockSpec((1, 1, 6 * D_MODEL), lambda i: (_mod_row(i, tm), 0, 0)),
            pl.BlockSpec((1, D_MODEL), const),
            pl.BlockSpec((D_MODEL, N_MAIN), const),
            pl.BlockSpec((D_MODEL, LANES), const),
            pl.BlockSpec((LANES, 2 * GLA_DK), const),
            pl.BlockSpec((1, 2 * GLA_DK), const),
        ],
        out_specs=[pl.BlockSpec((tm, n), row) for n, _ in outs],
        out_shape=[jax.ShapeDtypeStruct((N_TOK, n), dt) for n, dt in outs],
        compiler_params=_params("parallel"),
        name="mixin",
    )(xp, xs, mod3, g1, w_main, w_ab, w_up, b_up)


def _gla_kernel(*refs, seq_len, has_init, emit_state):
    q_ref, k_ref, v_ref, gf_ref, gb_ref, sg_ref, gn_ref = refs[:7]
    pos = 7
    if has_init:
        s0f_ref, s0b_ref = refs[pos:pos + 2]
        pos += 2
    pos += 1
    o_ref = refs[pos]
    pos += 1
    if emit_state:
        sf_ref, sb_ref = refs[pos:pos + 2]
        pos += 2
    oacc_ref, stf_ref, stb_ref, bf_ref, bb_ref = refs[pos:pos + 5]

    n_chunks = seq_len // CHUNK
    lane = lax.broadcasted_iota(jnp.int32, (1, GLA_DK), 1)
    head_mask = [(lane >= h * GLA_HK) & (lane < (h + 1) * GLA_HK) for h in range(GLA_HEADS)]
    srow = lax.broadcasted_iota(jnp.int32, (GLA_HEADS * CHUNK, CHUNK), 0) % CHUNK
    scol = lax.broadcasted_iota(jnp.int32, (GLA_HEADS * CHUNK, CHUNK), 1)
    ones_t = jnp.ones((CHUNK, GLA_HV), BF16)

    gi = lax.broadcasted_iota(jnp.int32, (CUM_ROWS, CUM_ROWS), 0)
    gj = lax.broadcasted_iota(jnp.int32, (CUM_ROWS, CUM_ROWS), 1)
    same_chunk = (gi // CHUNK) == (gj // CHUNK)
    tri_f = (same_chunk & (gi >= gj)).astype(BF16)
    tri_b = (same_chunk & (gi <= gj)).astype(BF16)
    for r in range(seq_len // CUM_ROWS):
        grp = slice(r * CUM_ROWS, (r + 1) * CUM_ROWS)
        for g_ref, b_ref, tri in ((gf_ref, bf_ref, tri_f), (gb_ref, bb_ref, tri_b)):
            g_hi, g_lo = _split_bf16(g_ref[grp, :])
            b_ref[grp, :] = _dot(tri, g_hi) + _dot(tri, g_lo)

    def chunk(c, g_ref, b_ref, st_ref, forward):
        if forward:
            causal = srow >= scol
            ref_row, last_row = CHUNK // 2 - 1, CHUNK - 1
        else:
            causal = srow <= scol
            ref_row, last_row = CHUNK // 2, 0
        r0 = pl.multiple_of(c * CHUNK, CHUNK)
        rows = pl.ds(r0, CHUNK)
        g = g_ref[rows, :]
        q = q_ref[rows, :]
        k = k_ref[rows, :]
        vb = v_ref[rows, :].astype(BF16)
        g_hi, g_lo = _split_bf16(g)
        b = b_ref[rows, :]
        b_mid = b[ref_row:ref_row + 1, :]
        b_end = b[last_row:last_row + 1, :]
        qt = q * jnp.exp(b - b_mid)
        kt = (k * jnp.exp(b_mid - b)).astype(BF16)
        qe = q * jnp.exp(b)
        kd = (k * jnp.exp(b_end - b)).astype(BF16)
        q_stack = jnp.concatenate([jnp.where(m, qt, 0.0) for m in head_mask], axis=0).astype(BF16)
        qe_stack = jnp.concatenate([jnp.where(m, qe, 0.0) for m in head_mask], axis=0).astype(BF16)
        scores = jnp.where(causal, _dot_nt(q_stack, kt), 0.0)
        o_intra = _dot(scores.astype(BF16), vb)
        state = st_ref[...]
        o_inter = _dot(qe_stack, state.astype(BF16))
        upd = _dot_tn(kd, vb)
        dec = jnp.exp(_dot_tn(g_hi, ones_t) + _dot_tn(g_lo, ones_t))
        for h in range(GLA_HEADS):
            hr = slice(h * CHUNK, (h + 1) * CHUNK)
            hv = slice(h * GLA_HV, (h + 1) * GLA_HV)
            oacc_ref[rows, hv] += o_intra[hr, hv] + o_inter[hr, :]
            st_ref[hr, :] = dec[hr, :] * state[hr, :] + upd[hr, hv]

    oacc_ref[...] = jnp.zeros_like(oacc_ref)
    if has_init:
        stf_ref[...] = s0f_ref[0].reshape(GLA_HEADS * GLA_HK, GLA_HV)
        stb_ref[...] = s0b_ref[0].reshape(GLA_HEADS * GLA_HK, GLA_HV)
    else:
        stf_ref[...] = jnp.zeros_like(stf_ref)
        stb_ref[...] = jnp.zeros_like(stb_ref)

    def step(it, carry):
        chunk(it, gf_ref, bf_ref, stf_ref, True)
        chunk(n_chunks - 1 - it, gb_ref, bb_ref, stb_ref, False)
        return carry

    lax.fori_loop(0, n_chunks, step, 0)
    if emit_state:
        sf_ref[0] = stf_ref[...].reshape(GLA_HEADS, GLA_HK, GLA_HV)
        sb_ref[0] = stb_ref[...].reshape(GLA_HEADS, GLA_HK, GLA_HV)

    o = oacc_ref[...]
    gn = gn_ref[...]
    sg = sg_ref[...]
    for h in range(GLA_HEADS):
        hv = slice(h * GLA_HV, (h + 1) * GLA_HV)
        oh = o[:, hv]
        ms = jnp.mean(oh * oh, axis=-1, keepdims=True)
        o_ref[:, hv] = ((oh * lax.rsqrt(ms + EPS)) * gn[:, hv] * sg[:, hv]).astype(o_ref.dtype)


def _gla(q, k, v, gf, gb, sg, gn, prev_out, *, seq_len, n_seq, first_block, init_states, emit_state):
    row = lambda i: (first_block + i, 0)
    const = lambda i: (0, 0)
    st_spec = pl.BlockSpec((1, GLA_HEADS, GLA_HK, GLA_HV), lambda i: (i, 0, 0, 0))
    in_specs = [
        pl.BlockSpec((seq_len, GLA_DK), row), pl.BlockSpec((seq_len, GLA_DK), row),
        pl.BlockSpec((seq_len, GLA_DV), row), pl.BlockSpec((seq_len, GLA_DK), row),
        pl.BlockSpec((seq_len, GLA_DK), row), pl.BlockSpec((seq_len, GLA_DV), row),
        pl.BlockSpec((1, GLA_DV), const),
    ]
    args = [q, k, v, gf, gb, sg, gn]
    if init_states is not None:
        in_specs += [st_spec, st_spec]
        args += list(init_states)
    in_specs.append(pl.BlockSpec(memory_space=pl.ANY))
    args.append(prev_out)
    out_specs = [pl.BlockSpec((seq_len, GLA_DV), row)]
    out_shape = [jax.ShapeDtypeStruct(prev_out.shape, prev_out.dtype)]
    if emit_state:
        out_specs += [st_spec, st_spec]
        out_shape += [jax.ShapeDtypeStruct((n_seq, GLA_HEADS, GLA_HK, GLA_HV), F32)] * 2
    return pl.pallas_call(
        functools.partial(_gla_kernel, seq_len=seq_len, has_init=init_states is not None, emit_state=emit_state),
        grid=(n_seq,),
        in_specs=in_specs,
        out_specs=out_specs,
        out_shape=out_shape,
        scratch_shapes=[pltpu.VMEM((seq_len, GLA_DV), F32), pltpu.VMEM((GLA_HEADS * GLA_HK, GLA_HV), F32),
                        pltpu.VMEM((GLA_HEADS * GLA_HK, GLA_HV), F32),
                        pltpu.VMEM((seq_len, GLA_DK), F32), pltpu.VMEM((seq_len, GLA_DK), F32)],
        input_output_aliases={len(args) - 1: 0},
        compiler_params=_params("parallel"),
        name="gla_lat" if init_states is not None else "gla_ctx",
    )(*args)


ROW_PITCH = GRID_W + 2 * (CONV_PAD + 1)


def _conv_kernel(u_ref, w_ref, b_ref, lng_ref, lnb_ref, pw_ref, prev_ref, o_ref, pad_ref, y_ref, *, latent, seq_len):
    del prev_ref
    half = CONV_CH // 2
    blk = GRID_W
    n_blk = seq_len // blk
    n_pad = pad_ref.shape[1]
    pad_ref[0] = jnp.zeros(pad_ref.shape[1:], F32)
    if latent:
        for r in range(n_blk):
            pad_ref[0, r * ROW_PITCH + CONV_PAD + 1:r * ROW_PITCH + CONV_PAD + 1 + blk, :] = u_ref[r * blk:(r + 1) * blk, 0:half]
    else:
        pad_ref[0, CONV_PAD + 1:CONV_PAD + 1 + seq_len, :] = u_ref[...]
    for s in range(1, SUBLANES):
        pad_ref[s, 0:n_pad - SUBLANES, :] = pad_ref[0, s:n_pad - SUBLANES + s, :]

    def window(first_row, tap):
        q, s = divmod(tap + 1, SUBLANES)
        return pad_ref[s, first_row + q * SUBLANES:first_row + q * SUBLANES + blk, :]

    bias = b_ref[...]
    lng = lng_ref[...]
    lnb = lnb_ref[...]
    for r in range(n_blk):
        if latent:
            acc_w = jnp.zeros((blk, half), F32)
            for t in range(CONV_K):
                acc_w += w_ref[t:t + 1, 0:half] * window(r * ROW_PITCH, t)
            acc_h = jnp.zeros((blk, half), F32)
            for r2 in range(n_blk):
                t = r2 - r + CONV_PAD
                acc_h += w_ref[t:t + 1, half:CONV_CH] * u_ref[r2 * blk:(r2 + 1) * blk, half:CONV_CH]
            y = jnp.concatenate([acc_w, acc_h], axis=-1) + bias
        else:
            acc = jnp.zeros((blk, CONV_CH), F32)
            for t in range(CONV_K):
                acc += w_ref[t:t + 1, :] * window(r * blk, t)
            y = acc + bias
        mu = jnp.mean(y, axis=-1, keepdims=True)
        yc = y - mu
        var = jnp.mean(yc * yc, axis=-1, keepdims=True)
        yn = (yc * lax.rsqrt(var + EPS)) * lng + lnb
        y_ref[r * blk:(r + 1) * blk, :] = (yn * jax.nn.sigmoid(yn)).astype(BF16)
    o_ref[...] = _dot(y_ref[...], pw_ref[...]).astype(o_ref.dtype)


def _conv(u, w, b, lng, lnb, pw, prev_out, *, seq_len, n_seq, first_block, latent):
    row = lambda i: (first_block + i, 0)
    const = lambda i: (0, 0)
    if latent:
        pad_shape = (SUBLANES, GRID_H * ROW_PITCH, CONV_CH // 2)
    else:
        pad_shape = (SUBLANES, seq_len + 2 * (CONV_PAD + 1), CONV_CH)
    return pl.pallas_call(
        functools.partial(_conv_kernel, latent=latent, seq_len=seq_len),
        grid=(n_seq,),
        in_specs=[
            pl.BlockSpec((seq_len, CONV_CH), row),
            pl.BlockSpec((CONV_K + 1, CONV_CH), const),
            pl.BlockSpec((1, CONV_CH), const), pl.BlockSpec((1, CONV_CH), const), pl.BlockSpec((1, CONV_CH), const),
            pl.BlockSpec((CONV_CH, CONV_CH), const),
            pl.BlockSpec(memory_space=pl.ANY),
        ],
        out_specs=pl.BlockSpec((seq_len, CONV_CH), row),
        out_shape=jax.ShapeDtypeStruct(prev_out.shape, prev_out.dtype),
        scratch_shapes=[pltpu.VMEM(pad_shape, F32), pltpu.VMEM((seq_len, CONV_CH), BF16)],
        input_output_aliases={6: 0},
        compiler_params=_params("parallel"),
        name="conv_lat" if latent else "conv_ctx",
    )(u, w, b, lng, lnb, pw, prev_out)


def _mid_kernel(mo_ref, uu_ref, xp_ref, xs_ref, mod_ref, g2_ref, wo1_ref, wo2_ref, wqt_ref, keys_ref,
                x1_ref, h2t_ref, st_ref):
    mod = mod_ref[0]
    gate1 = mod[:, 2 * D_MODEL:3 * D_MODEL]
    shift2 = mod[:, 3 * D_MODEL:4 * D_MODEL]
    scale2 = mod[:, 4 * D_MODEL:5 * D_MODEL]
    m = _dot(mo_ref[...], wo1_ref[...]) + _dot(uu_ref[...], wo2_ref[...])
    x1 = _token_tile(xp_ref, xs_ref, TOK_TILE) + gate1 * m
    x1_ref[...] = x1
    ms = jnp.mean(x1 * x1, axis=-1, keepdims=True)
    h2 = (x1 * lax.rsqrt(ms + EPS)) * g2_ref[...]
    h2 = h2 * (1.0 + scale2) + shift2
    h2t = h2.T.astype(BF16)
    h2t_ref[...] = h2t
    qt = _dot(wqt_ref[...], h2t)
    for hp in range(2 * PEER_HEADS):
        rows = slice(hp * PEER_NKEYS, (hp + 1) * PEER_NKEYS)
        st_ref[rows, :] = _dot(keys_ref[hp], qt[rows, :].astype(BF16))


def _mid(mo, uu, xp, xs, mod3, g2, wo1, wo2, wqt, keys):
    tm = TOK_TILE
    row = lambda i: (i, 0)
    col = lambda i: (0, i)
    const = lambda i: (0, 0)
    n_q = PEER_HEADS * PEER_DQ
    return pl.pallas_call(
        _mid_kernel,
        grid=(N_TOK // tm,),
        in_specs=[
            pl.BlockSpec((tm, GLA_DV), row), pl.BlockSpec((tm, CONV_CH), row), _ctx_spec(tm), _lat_spec(tm),
            pl.BlockSpec((1, 1, 6 * D_MODEL), lambda i: (_mod_row(i, tm), 0, 0)),
            pl.BlockSpec((1, D_MODEL), const),
            pl.BlockSpec((GLA_DV, D_MODEL), const), pl.BlockSpec((CONV_CH, D_MODEL), const),
            pl.BlockSpec((n_q, D_MODEL), const),
            pl.BlockSpec((2 * PEER_HEADS, PEER_NKEYS, PEER_DQ // 2), lambda i: (0, 0, 0)),
        ],
        out_specs=[pl.BlockSpec((tm, D_MODEL), row), pl.BlockSpec((D_MODEL, tm), col), pl.BlockSpec((n_q, tm), col)],
        out_shape=[jax.ShapeDtypeStruct((N_TOK, D_MODEL), F32), jax.ShapeDtypeStruct((D_MODEL, N_TOK), BF16),
                   jax.ShapeDtypeStruct((n_q, N_TOK), F32)],
        compiler_params=_params("parallel"),
        name="mid",
    )(mo, uu, xp, xs, mod3, g2, wo1, wo2, wqt, keys)


N_CAND_ROWS = 80


def _top16(s, key_iota, sv_ref, exact, want_rank):
    rank = jnp.full(s.shape, PEER_TOPK, jnp.int32) if want_rank else None
    for a in range(PEER_TOPK):
        m = jnp.max(s, axis=0, keepdims=True)
        hit = s == m
        if exact:
            first = jnp.min(jnp.where(hit, key_iota, PEER_NKEYS), axis=0, keepdims=True)
            hit = key_iota == first
        if want_rank:
            rank = jnp.where(hit, a, rank)
        s = jnp.where(hit, -jnp.inf, s)
        sv_ref[a:a + 1, :] = m
    return rank, s


def _count_true(mask):
    return jnp.sum(jnp.where(mask, 1.0, 0.0), axis=0, keepdims=True)


def _select_head(h, bad, st_ref, pa_ref, pb_ref, sv1_ref, sv2_ref, cand_ref, *, exact):
    key_iota = lax.broadcasted_iota(jnp.int32, (PEER_NKEYS, SEL_TILE), 0)
    crow = lax.broadcasted_iota(jnp.int32, (N_CAND_ROWS, SEL_TILE), 0)
    flat = jnp.where(crow < 16, crow * 16,
                     jnp.where(crow < 72, ((crow - 16) % 8) * 16 + (crow - 16) // 8 + 1, crow - 72 + 8))
    base = pl.multiple_of(h * 2 * PEER_NKEYS, 2 * PEER_NKEYS)
    s1 = st_ref[pl.ds(base, PEER_NKEYS), :]
    s2 = st_ref[pl.ds(base + PEER_NKEYS, PEER_NKEYS), :]
    r1, _ = _top16(s1, key_iota, sv1_ref, exact, True)
    r2, s2_left = _top16(s2, key_iota, sv2_ref, exact, exact)
    top1 = sv1_ref[0:1, :]
    top2 = sv2_ref[0:1, :]
    cand_ref[0:16, :] = sv1_ref[...] + top2
    for b in range(1, 8):
        cand_ref[8 + 8 * b:16 + 8 * b, :] = sv1_ref[0:8, :] + sv2_ref[b:b + 1, :]
    cand_ref[72:80, :] = top1 + sv2_ref[8:16, :]
    cand = cand_ref[...]
    cmax = top1 + top2
    sel = jnp.zeros(cand.shape, jnp.bool_)
    z = jnp.zeros((1, SEL_TILE), F32)
    for _ in range(PEER_TOPK):
        m = jnp.max(cand, axis=0, keepdims=True)
        hit = cand == m
        if exact:
            first = jnp.min(jnp.where(hit, flat, 256), axis=0, keepdims=True)
            hit = flat == first
        sel = sel | hit
        cand = jnp.where(hit, -jnp.inf, cand)
        z = z + jnp.exp(m - cmax)
    self32 = jnp.where(sel, 1.0, 0.0)
    counts = [jnp.sum(self32[0:16, :], axis=0, keepdims=True)]
    for b in range(1, 8):
        counts.append(jnp.sum(self32[8 + 8 * b:16 + 8 * b, :], axis=0, keepdims=True))
    for b in range(8, 16):
        counts.append(self32[72 + b - 8:72 + b - 7, :])
    n2 = jnp.zeros((PEER_NKEYS, SEL_TILE), F32)
    for b in range(PEER_TOPK):
        is_b = (r2 == b) if exact else (s2 == sv2_ref[b:b + 1, :])
        n2 = jnp.where(is_b, counts[b], n2)
    out = pl.multiple_of(h * SEL_ROWS_PER_HEAD, SEL_ROWS_PER_HEAD)
    pa_ref[pl.ds(out, PEER_NKEYS), :] = r1.astype(F32)
    pa_ref[pl.ds(out + PEER_NKEYS, PEER_NKEYS), :] = jnp.exp(s1 - top1)
    pb_ref[pl.ds(out, PEER_NKEYS), :] = n2.astype(BF16)
    pb_ref[pl.ds(out + PEER_NKEYS, PEER_NKEYS), :] = (jnp.exp(s2 - top2) / z).astype(BF16)
    if exact:
        return bad
    n_taken = _count_true(r1 < PEER_TOPK) + _count_true(s2_left == -jnp.inf) + jnp.sum(self32, axis=0, keepdims=True)
    return jnp.maximum(bad, jnp.where(n_taken == 3.0 * PEER_TOPK, 0.0, 1.0))


HEADS_PER_TRIP = 2


def _select_kernel(st_ref, pa_ref, pb_ref, sv1_ref, sv2_ref, cand_ref):
    def trip(exact):
        def body(t, bad):
            for k in range(HEADS_PER_TRIP):
                bad = _select_head(t * HEADS_PER_TRIP + k, bad, st_ref, pa_ref, pb_ref,
                                   sv1_ref.at[k], sv2_ref.at[k], cand_ref.at[k], exact=exact)
            return bad
        return body

    bad = lax.fori_loop(0, PEER_HEADS // HEADS_PER_TRIP, trip(False), jnp.zeros((1, SEL_TILE), F32))

    @pl.when(jnp.max(bad) > 0.0)
    def _():
        lax.fori_loop(0, PEER_HEADS // HEADS_PER_TRIP, trip(True), bad)


def _select(st):
    n_q = PEER_HEADS * PEER_DQ
    n_pe = PEER_HEADS * SEL_ROWS_PER_HEAD
    col = lambda i: (0, i)
    return pl.pallas_call(
        _select_kernel,
        grid=(N_TOK // SEL_TILE,),
        in_specs=[pl.BlockSpec((n_q, SEL_TILE), col)],
        out_specs=[pl.BlockSpec((n_pe, SEL_TILE), col), pl.BlockSpec((n_pe, SEL_TILE), col)],
        out_shape=[jax.ShapeDtypeStruct((n_pe, N_TOK), F32), jax.ShapeDtypeStruct((n_pe, N_TOK), BF16)],
        scratch_shapes=[pltpu.VMEM((HEADS_PER_TRIP, PEER_TOPK, SEL_TILE), F32),
                        pltpu.VMEM((HEADS_PER_TRIP, PEER_TOPK, SEL_TILE), F32),
                        pltpu.VMEM((HEADS_PER_TRIP, N_CAND_ROWS, SEL_TILE), F32)],
        compiler_params=_params("parallel"),
        name="select",
    )(st)


N_EXP_TILES = PEER_N // EXP_TILE
N_EXP_TOK_TILES = N_TOK // EXP_TOK_TILE
N_EXP_WORK = N_EXP_TOK_TILES * N_EXP_TILES
N_EXP_STEPS = N_EXP_WORK + 2


def _experts_kernel(h2t_ref, pa_ref, pb_ref, u_ref, vt_ref, x1_ref, mod_ref, gfin_ref, yp_ref, ys_ref,
                    acc_ref, a_ref, c_ref):
    s = pl.program_id(0)
    item_gate = jnp.clip(s - 1, 0, N_EXP_WORK - 1)
    e_gate = item_gate % N_EXP_TILES
    e_out = jnp.maximum(s - 2, 0) % N_EXP_TILES
    tile3 = (PEER_NKEYS // BF16_ROWS, BF16_ROWS, LANES)

    @pl.when(s == 0)
    def _():
        a_ref[...] = jnp.zeros_like(a_ref)
        c_ref[...] = jnp.zeros_like(c_ref)

    @pl.when((s == 0) | ((s >= 2) & (e_out == 0)))
    def _():
        acc_ref[...] = jnp.zeros_like(acc_ref)

    def token_group(grp):
        cols = slice(grp * EXP_TOK_GROUP, (grp + 1) * EXP_TOK_GROUP)
        acc_ref[:, cols] += _dot(vt_ref[...], c_ref[:, cols])
        for ib in range(EXP_TILE // PEER_NKEYS):
            i = e_gate * (EXP_TILE // PEER_NKEYS) + ib
            rows = slice(ib * PEER_NKEYS, (ib + 1) * PEER_NKEYS)
            rank_rows = [pa_ref[pl.ds(h * SEL_ROWS_PER_HEAD + i, 1), :] for h in range(PEER_HEADS)]
            e1_rows = [pa_ref[pl.ds(h * SEL_ROWS_PER_HEAD + PEER_NKEYS + i, 1), :] for h in range(PEER_HEADS)]
            for lc in range(EXP_TOK_GROUP // LANES):
                first = grp * EXP_TOK_GROUP + lc * LANES
                lanes = slice(first, first + LANES)
                a = a_ref[rows, lanes]
                act = (0.5 * a * (1.0 + lax.erf(a * (2.0 ** -0.5)))).astype(BF16).reshape(tile3)
                w = jnp.zeros(tile3, BF16)
                for h in range(PEER_HEADS):
                    base = h * SEL_ROWS_PER_HEAD
                    rank_i = jnp.broadcast_to(rank_rows[h][:, lanes], (BF16_ROWS, LANES)).astype(BF16)
                    e1_i = jnp.broadcast_to(e1_rows[h][:, lanes], (BF16_ROWS, LANES)).astype(BF16)
                    n2 = pb_ref[base:base + PEER_NKEYS, lanes].reshape(tile3)
                    e2 = pb_ref[base + PEER_NKEYS:base + 2 * PEER_NKEYS, lanes].reshape(tile3)
                    w = w + jnp.where(rank_i[None] < n2, e1_i[None] * e2, jnp.zeros((), BF16))
                c_ref[rows, lanes] = (act * w).reshape(PEER_NKEYS, LANES)
        a_ref[:, cols] = _dot(u_ref[...], h2t_ref[:, cols])

    for grp in range(EXP_TOK_TILE // EXP_TOK_GROUP):
        token_group(grp)

    @pl.when((s >= 2) & (e_out == N_EXP_TILES - 1))
    def _():
        mod = mod_ref[0]
        gate2 = mod[:, 5 * D_MODEL:6 * D_MODEL]
        y = x1_ref[...] + gate2 * acc_ref[...].T
        ms = jnp.mean(y * y, axis=-1, keepdims=True)
        y = (y * lax.rsqrt(ms + EPS)) * gfin_ref[...]
        is_ctx = jnp.maximum(s - 2, 0) // N_EXP_TILES < N_CTX_TOK // EXP_TOK_TILE

        @pl.when(is_ctx)
        def _():
            yp_ref[...] = y

        @pl.when(jnp.logical_not(is_ctx))
        def _():
            ys_ref[...] = y


def _experts(h2t, pa, pb, u_bf, vt_bf, x1, mod3, gfin):
    tm = EXP_TOK_TILE
    n_pe = PEER_HEADS * SEL_ROWS_PER_HEAD
    item = lambda s, lag: jnp.clip(s - lag, 0, N_EXP_WORK - 1)
    tok = lambda s, lag: item(s, lag) // N_EXP_TILES
    exp = lambda s, lag: item(s, lag) % N_EXP_TILES
    n_ctx_tiles = N_CTX_TOK // tm
    return pl.pallas_call(
        _experts_kernel,
        grid=(N_EXP_STEPS,),
        in_specs=[
            pl.BlockSpec((D_MODEL, tm), lambda s: (0, tok(s, 0))),
            pl.BlockSpec((n_pe, tm), lambda s: (0, tok(s, 1))),
            pl.BlockSpec((n_pe, tm), lambda s: (0, tok(s, 1))),
            pl.BlockSpec((EXP_TILE, D_MODEL), lambda s: (exp(s, 0), 0)),
            pl.BlockSpec((D_MODEL, EXP_TILE), lambda s: (0, exp(s, 2))),
            pl.BlockSpec((tm, D_MODEL), lambda s: (tok(s, 2), 0)),
            pl.BlockSpec((1, 1, 6 * D_MODEL), lambda s: (_mod_row(tok(s, 2), tm), 0, 0)),
            pl.BlockSpec((1, D_MODEL), lambda s: (0, 0)),
        ],
        out_specs=[pl.BlockSpec((tm, D_MODEL), lambda s: (jnp.minimum(tok(s, 2), n_ctx_tiles - 1), 0)),
                   pl.BlockSpec((tm, D_MODEL), lambda s: (jnp.maximum(tok(s, 2) - n_ctx_tiles, 0), 0))],
        out_shape=[jax.ShapeDtypeStruct((N_CTX_TOK, D_MODEL), F32), jax.ShapeDtypeStruct((N_LAT_TOK, D_MODEL), F32)],
        scratch_shapes=[pltpu.VMEM((D_MODEL, tm), F32), pltpu.VMEM((EXP_TILE, tm), F32),
                        pltpu.VMEM((EXP_TILE, tm), BF16)],
        compiler_params=_params("arbitrary"),
        name="experts",
    )(h2t, pa, pb, u_bf, vt_bf, x1, mod3, gfin)


def kernel(x_prompt, x_sample, c, state_gla_fwd, state_gla_bwd, c_ctx, norm1_g, w_mod, b_mod, w_in, w_af_up, b_af,
           w_ab_up, b_ab, gla_norm_g, conv_dw_w, conv_dw_b, conv_ln_g, conv_ln_b, conv_pw2, w_out, norm2_g, peer_wq,
           peer_subkeys, peer_u, peer_v, final_norm_g):
    depth = w_in.shape[0]
    assert depth == 1
    l = 0
    xp = x_prompt.reshape(N_CTX_TOK, D_MODEL)
    xs = x_sample.reshape(N_LAT_TOK, D_MODEL)

    c_all = jnp.zeros((N_MOD_PAD, D_MODEL), F32).at[0].set(c_ctx).at[1:N_MOD_ROWS].set(c)
    w = w_in[l]
    w_main = jnp.concatenate([w[:, 0:1536], w[:, 1568:2592]], axis=1).astype(BF16)
    w_ab = jnp.zeros((D_MODEL, LANES), F32).at[:, 0:2 * GLA_RANK].set(w[:, 1536:1568]).astype(BF16)
    w_up = jnp.zeros((LANES, 2 * GLA_DK), F32)
    w_up = w_up.at[0:GLA_RANK, 0:GLA_DK].set(w_af_up[l]).at[GLA_RANK:2 * GLA_RANK, GLA_DK:].set(w_ab_up[l]).astype(BF16)
    b_up = jnp.concatenate([b_af[l], b_ab[l]])[None, :]
    conv_w = jnp.zeros((CONV_K + 1, CONV_CH), F32).at[0:CONV_K].set(conv_dw_w[l])
    wo1 = w_out[l][0:GLA_DV].astype(BF16)
    wo2 = w_out[l][GLA_DV:].astype(BF16)
    wqt = peer_wq[l].T.astype(BF16)
    keys = peer_subkeys[l].reshape(2 * PEER_HEADS, PEER_NKEYS, PEER_DQ // 2).astype(BF16)
    u_bf = peer_u[l].astype(BF16)
    vt_bf = peer_v[l].T.astype(BF16)

    mod = _modulation(c_all, w_mod[l], b_mod[l][None, :])
    mod3 = mod[:, None, :]

    q, k, v, gf, gb, sg, u = _mixin(xp, xs, mod3, norm1_g[l][None, :], w_main, w_ab, w_up, b_up)

    gn = gla_norm_g[l][None, :]
    mo0 = jnp.zeros((N_TOK, GLA_DV), BF16)
    mo1, new_f, new_b = _gla(q, k, v, gf, gb, sg, gn, mo0, seq_len=CTX_LEN, n_seq=N_CTX_SEQ, first_block=0,
                             init_states=None, emit_state=True)
    (mo,) = _gla(q, k, v, gf, gb, sg, gn, mo1, seq_len=LAT_LEN, n_seq=N_LAT_SEQ,
                 first_block=N_CTX_TOK // LAT_LEN, init_states=(state_gla_fwd[:, l], state_gla_bwd[:, l]),
                 emit_state=False)

    conv_args = (conv_w, conv_dw_b[l][None, :], conv_ln_g[l][None, :], conv_ln_b[l][None, :], conv_pw2[l].astype(BF16))
    uu0 = jnp.zeros((N_TOK, CONV_CH), BF16)
    uu1 = _conv(u, *conv_args, uu0, seq_len=CTX_LEN, n_seq=N_CTX_SEQ, first_block=0, latent=False)
    uu = _conv(u, *conv_args, uu1, seq_len=LAT_LEN, n_seq=N_LAT_SEQ, first_block=N_CTX_TOK // LAT_LEN, latent=True)

    x1, h2t, st = _mid(mo, uu, xp, xs, mod3, norm2_g[l][None, :], wo1, wo2, wqt, keys)
    pa, pb = _select(st)
    yp, ys = _experts(h2t, pa, pb, u_bf, vt_bf, x1, mod3, final_norm_g[None, :])

    y_prompt = yp.reshape(N_CTX_SEQ, CTX_LEN, D_MODEL)
    y_sample = ys.reshape(N_LAT_SEQ, LAT_LEN, D_MODEL)
    return (y_prompt, y_sample, new_f[:, None], new_b[:, None])
```

```python
import functools

import jax
import jax.numpy as jnp
from jax import lax
from jax.experimental import pallas as pl
from jax.experimental.pallas import tpu as pltpu

F32 = jnp.float32
BF16 = jnp.bfloat16

D_MODEL = 1024
N_CTX_SEQ = 32
CTX_LEN = 256
N_LAT_SEQ = 8
LAT_LEN = 1024
N_CTX_TOK = N_CTX_SEQ * CTX_LEN
N_LAT_TOK = N_LAT_SEQ * LAT_LEN
N_TOK = N_CTX_TOK + N_LAT_TOK
GRID_W = 64
GRID_H = LAT_LEN // GRID_W
GLA_HEADS = 4
GLA_DK = 256
GLA_DV = 512
GLA_HK = 64
GLA_HV = 128
GLA_RANK = 16
GLA_NORMALIZER = 16.0
CHUNK = 64
CUM_ROWS = 256
CONV_CH = 512
CONV_K = 31
CONV_PAD = CONV_K // 2
PEER_HEADS = 8
PEER_NKEYS = 128
PEER_N = PEER_NKEYS * PEER_NKEYS
PEER_DQ = 256
PEER_TOPK = 16
EPS = 1e-6
N_MOD_ROWS = 1 + N_LAT_SEQ
N_MOD_PAD = 16

LANES = 128
SUBLANES = 8
VMEM_LIMIT_BYTES = 56 * 1024 * 1024

TOK_TILE = 512
SEL_TILE = LANES
EXP_TOK_TILE = 512
EXP_TOK_GROUP = 256
EXP_TILE = 1024
SEL_ROWS_PER_HEAD = 2 * PEER_NKEYS
BF16_ROWS = 16


def _params(*semantics):
    return pltpu.CompilerParams(dimension_semantics=semantics, vmem_limit_bytes=VMEM_LIMIT_BYTES)


def _mod_row(tile, tile_rows):
    first = tile * tile_rows
    return jnp.where(first < N_CTX_TOK, 0, 1 + (first - N_CTX_TOK) // LAT_LEN)


def _ctx_spec(tile_rows):
    n_ctx_tiles = N_CTX_TOK // tile_rows
    return pl.BlockSpec((tile_rows, D_MODEL), lambda i: (jnp.minimum(i, n_ctx_tiles - 1), 0))


def _lat_spec(tile_rows):
    n_ctx_tiles = N_CTX_TOK // tile_rows
    return pl.BlockSpec((tile_rows, D_MODEL), lambda i: (jnp.maximum(i - n_ctx_tiles, 0), 0))


def _token_tile(xp_ref, xs_ref, tile_rows):
    is_ctx = pl.program_id(0) < N_CTX_TOK // tile_rows
    return jnp.where(is_ctx, xp_ref[...], xs_ref[...])


def _split_bf16(a):
    hi = a.astype(BF16)
    lo = (a - hi.astype(F32)).astype(BF16)
    return hi, lo


def _dot(a, b):
    return jnp.dot(a, b, preferred_element_type=F32)


def _dot_nt(a, b):
    return lax.dot_general(a, b, (((1,), (1,)), ((), ())), preferred_element_type=F32)


def _dot_tn(a, b):
    return lax.dot_general(a, b, (((0,), (0,)), ((), ())), preferred_element_type=F32)


def _mod_kernel(c_ref, w_ref, b_ref, o_ref):
    c = c_ref[...]
    a = c * jax.nn.sigmoid(c)
    a_hi, a_lo = _split_bf16(a)
    w_hi, w_lo = _split_bf16(w_ref[...])
    o_ref[...] = _dot(a_hi, w_hi) + _dot(a_hi, w_lo) + _dot(a_lo, w_hi) + b_ref[...]


def _modulation(c_all, w_mod, b_mod):
    n_out = w_mod.shape[1]
    tile = 1024
    return pl.pallas_call(
        _mod_kernel,
        grid=(n_out // tile,),
        in_specs=[
            pl.BlockSpec((N_MOD_PAD, D_MODEL), lambda j: (0, 0)),
            pl.BlockSpec((D_MODEL, tile), lambda j: (0, j)),
            pl.BlockSpec((1, tile), lambda j: (0, j)),
        ],
        out_specs=pl.BlockSpec((N_MOD_PAD, tile), lambda j: (0, j)),
        out_shape=jax.ShapeDtypeStruct((N_MOD_PAD, n_out), F32),
        compiler_params=_params("parallel"),
        name="mod",
    )(c_all, w_mod, b_mod)


N_MAIN = 2 * GLA_DK + 2 * GLA_DV + 2 * CONV_CH


def _mixin_kernel(xp_ref, xs_ref, mod_ref, g1_ref, wm_ref, wab_ref, wup_ref, bup_ref,
                  q_ref, k_ref, v_ref, gf_ref, gb_ref, sg_ref, u_ref):
    x = _token_tile(xp_ref, xs_ref, TOK_TILE)
    mod = mod_ref[0]
    shift1 = mod[:, 0:D_MODEL]
    scale1 = mod[:, D_MODEL:2 * D_MODEL]
    ms = jnp.mean(x * x, axis=-1, keepdims=True)
    h = (x * lax.rsqrt(ms + EPS)) * g1_ref[...]
    h = h * (1.0 + scale1) + shift1
    hb = h.astype(BF16)
    p = _dot(hb, wm_ref[...])
    q_ref[...] = p[:, 0:256] * (GLA_HK ** -0.5)
    k_ref[...] = p[:, 256:512]
    v_ref[...] = p[:, 512:1024]
    g = p[:, 1024:1536]
    sg_ref[...] = g * jax.nn.sigmoid(g)
    ca = p[:, 1536:2048]
    cb = p[:, 2048:2560]
    u_ref[...] = ca * jax.nn.sigmoid(cb)
    a = _dot(hb, wab_ref[...])
    z = _dot(a.astype(BF16), wup_ref[...]) + bup_ref[...]
    ls = jnp.minimum(z, 0.0) - jnp.log(1.0 + jnp.exp(-jnp.abs(z)))
    gl = ls * (1.0 / GLA_NORMALIZER)
    gf_ref[...] = gl[:, 0:GLA_DK]
    gb_ref[...] = gl[:, GLA_DK:2 * GLA_DK]


def _mixin(xp, xs, mod3, g1, w_main, w_ab, w_up, b_up):
    tm = TOK_TILE
    row = lambda i: (i, 0)
    const = lambda i: (0, 0)
    outs = [(GLA_DK, F32), (GLA_DK, F32), (GLA_DV, F32), (GLA_DK, F32), (GLA_DK, F32),
            (GLA_DV, F32), (CONV_CH, F32)]
    return pl.pallas_call(
        _mixin_kernel,
        grid=(N_TOK // tm,),
        in_specs=[
            _ctx_spec(tm), _lat_spec(tm),
            pl.BlockSpec((1, 1, 6 * D_MODEL), lambda i: (_mod_row(i, tm), 0, 0)),
            pl.BlockSpec((1, D_MODEL), const),
            pl.BlockSpec((D_MODEL, N_MAIN), const),
            pl.BlockSpec((D_MODEL, LANES), const),
            pl.BlockSpec((LANES, 2 * GLA_DK), const),
            pl.BlockSpec((1, 2 * GLA_DK), const),
        ],
        out_specs=[pl.BlockSpec((tm, n), row) for n, _ in outs],
        out_shape=[jax.ShapeDtypeStruct((N_TOK, n), dt) for n, dt in outs],
        compiler_params=_params("parallel"),
        name="mixin",
    )(xp, xs, mod3, g1, w_main, w_ab, w_up, b_up)


def _gla_kernel(*refs, seq_len, has_init, emit_state):
    q_ref, k_ref, v_ref, gf_ref, gb_ref, sg_ref, gn_ref = refs[:7]
    pos = 7
    if has_init:
        s0f_ref, s0b_ref = refs[pos:pos + 2]
        pos += 2
    pos += 1
    o_ref = refs[pos]
    pos += 1
    if emit_state:
        sf_ref, sb_ref = refs[pos:pos + 2]
        pos += 2
    oacc_ref, stf_ref, stb_ref, bf_ref, bb_ref = refs[pos:pos + 5]

    n_chunks = seq_len // CHUNK
    lane = lax.broadcasted_iota(jnp.int32, (1, GLA_DK), 1)
    head_mask = [(lane >= h * GLA_HK) & (lane < (h + 1) * GLA_HK) for h in range(GLA_HEADS)]
    srow = lax.broadcasted_iota(jnp.int32, (GLA_HEADS * CHUNK, CHUNK), 0) % CHUNK
    scol = lax.broadcasted_iota(jnp.int32, (GLA_HEADS * CHUNK, CHUNK), 1)
    ones_t = jnp.ones((CHUNK, GLA_HV), BF16)

    gi = lax.broadcasted_iota(jnp.int32, (CUM_ROWS, CUM_ROWS), 0)
    gj = lax.broadcasted_iota(jnp.int32, (CUM_ROWS, CUM_ROWS), 1)
    same_chunk = (gi // CHUNK) == (gj // CHUNK)
    tri_f = (same_chunk & (gi >= gj)).astype(BF16)
    tri_b = (same_chunk & (gi <= gj)).astype(BF16)
    for r in range(seq_len // CUM_ROWS):
        grp = slice(r * CUM_ROWS, (r + 1) * CUM_ROWS)
        for g_ref, b_ref, tri in ((gf_ref, bf_ref, tri_f), (gb_ref, bb_ref, tri_b)):
            g_hi, g_lo = _split_bf16(g_ref[grp, :])
            b_ref[grp, :] = _dot(tri, g_hi) + _dot(tri, g_lo)

    def chunk(c, g_ref, b_ref, st_ref, forward):
        if forward:
            causal = srow >= scol
            ref_row, last_row = CHUNK // 2 - 1, CHUNK - 1
        else:
            causal = srow <= scol
            ref_row, last_row = CHUNK // 2, 0
        r0 = pl.multiple_of(c * CHUNK, CHUNK)
        rows = pl.ds(r0, CHUNK)
        g = g_ref[rows, :]
        q = q_ref[rows, :]
        k = k_ref[rows, :]
        vb = v_ref[rows, :].astype(BF16)
        g_hi, g_lo = _split_bf16(g)
        b = b_ref[rows, :]
        b_mid = b[ref_row:ref_row + 1, :]
        b_end = b[last_row:last_row + 1, :]
        qt = q * jnp.exp(b - b_mid)
        kt = (k * jnp.exp(b_mid - b)).astype(BF16)
        qe = q * jnp.exp(b)
        kd = (k * jnp.exp(b_end - b)).astype(BF16)
        q_stack = jnp.concatenate([jnp.where(m, qt, 0.0) for m in head_mask], axis=0).astype(BF16)
        qe_stack = jnp.concatenate([jnp.where(m, qe, 0.0) for m in head_mask], axis=0).astype(BF16)
        scores = jnp.where(causal, _dot_nt(q_stack, kt), 0.0)
        o_intra = _dot(scores.astype(BF16), vb)
        state = st_ref[...]
        o_inter = _dot(qe_stack, state.astype(BF16))
        upd = _dot_tn(kd, vb)
        dec = jnp.exp(_dot_tn(g_hi, ones_t) + _dot_tn(g_lo, ones_t))
        for h in range(GLA_HEADS):
            hr = slice(h * CHUNK, (h + 1) * CHUNK)
            hv = slice(h * GLA_HV, (h + 1) * GLA_HV)
            oacc_ref[rows, hv] += o_intra[hr, hv] + o_inter[hr, :]
            st_ref[hr, :] = dec[hr, :] * state[hr, :] + upd[hr, hv]

    oacc_ref[...] = jnp.zeros_like(oacc_ref)
    if has_init:
        stf_ref[...] = s0f_ref[0].reshape(GLA_HEADS * GLA_HK, GLA_HV)
        stb_ref[...] = s0b_ref[0].reshape(GLA_HEADS * GLA_HK, GLA_HV)
    else:
        stf_ref[...] = jnp.zeros_like(stf_ref)
        stb_ref[...] = jnp.zeros_like(stb_ref)

    def step(it, carry):
        chunk(it, gf_ref, bf_ref, stf_ref, True)
        chunk(n_chunks - 1 - it, gb_ref, bb_ref, stb_ref, False)
        return carry

    lax.fori_loop(0, n_chunks, step, 0)
    if emit_state:
        sf_ref[0] = stf_ref[...].reshape(GLA_HEADS, GLA_HK, GLA_HV)
        sb_ref[0] = stb_ref[...].reshape(GLA_HEADS, GLA_HK, GLA_HV)

    o = oacc_ref[...]
    gn = gn_ref[...]
    sg = sg_ref[...]
    for h in range(GLA_HEADS):
        hv = slice(h * GLA_HV, (h + 1) * GLA_HV)
        oh = o[:, hv]
        ms = jnp.mean(oh * oh, axis=-1, keepdims=True)
        o_ref[:, hv] = ((oh * lax.rsqrt(ms + EPS)) * gn[:, hv] * sg[:, hv]).astype(o_ref.dtype)


def _gla(q, k, v, gf, gb, sg, gn, prev_out, *, seq_len, n_seq, first_block, init_states, emit_state):
    row = lambda i: (first_block + i, 0)
    const = lambda i: (0, 0)
    st_spec = pl.BlockSpec((1, GLA_HEADS, GLA_HK, GLA_HV), lambda i: (i, 0, 0, 0))
    in_specs = [
        pl.BlockSpec((seq_len, GLA_DK), row), pl.BlockSpec((seq_len, GLA_DK), row),
        pl.BlockSpec((seq_len, GLA_DV), row), pl.BlockSpec((seq_len, GLA_DK), row),
        pl.BlockSpec((seq_len, GLA_DK), row), pl.BlockSpec((seq_len, GLA_DV), row),
        pl.BlockSpec((1, GLA_DV), const),
    ]
    args = [q, k, v, gf, gb, sg, gn]
    if init_states is not None:
        in_specs += [st_spec, st_spec]
        args += list(init_states)
    in_specs.append(pl.BlockSpec(memory_space=pl.ANY))
    args.append(prev_out)
    out_specs = [pl.BlockSpec((seq_len, GLA_DV), row)]
    out_shape = [jax.ShapeDtypeStruct(prev_out.shape, prev_out.dtype)]
    if emit_state:
        out_specs += [st_spec, st_spec]
        out_shape += [jax.ShapeDtypeStruct((n_seq, GLA_HEADS, GLA_HK, GLA_HV), F32)] * 2
    return pl.pallas_call(
        functools.partial(_gla_kernel, seq_len=seq_len, has_init=init_states is not None, emit_state=emit_state),
        grid=(n_seq,),
        in_specs=in_specs,
        out_specs=out_specs,
        out_shape=out_shape,
        scratch_shapes=[pltpu.VMEM((seq_len, GLA_DV), F32), pltpu.VMEM((GLA_HEADS * GLA_HK, GLA_HV), F32),
                        pltpu.VMEM((GLA_HEADS * GLA_HK, GLA_HV), F32),
                        pltpu.VMEM((seq_len, GLA_DK), F32), pltpu.VMEM((seq_len, GLA_DK), F32)],
        input_output_aliases={len(args) - 1: 0},
        compiler_params=_params("parallel"),
        name="gla_lat" if init_states is not None else "gla_ctx",
    )(*args)


ROW_PITCH = GRID_W + 2 * (CONV_PAD + 1)


def _conv_kernel(u_ref, w_ref, b_ref, lng_ref, lnb_ref, pw_ref, prev_ref, o_ref, pad_ref, y_ref, *, latent, seq_len):
    del prev_ref
    half = CONV_CH // 2
    blk = GRID_W
    n_blk = seq_len // blk
    n_pad = pad_ref.shape[1]
    pad_ref[0] = jnp.zeros(pad_ref.shape[1:], F32)
    if latent:
        for r in range(n_blk):
            pad_ref[0, r * ROW_PITCH + CONV_PAD + 1:r * ROW_PITCH + CONV_PAD + 1 + blk, :] = u_ref[r * blk:(r + 1) * blk, 0:half]
    else:
        pad_ref[0, CONV_PAD + 1:CONV_PAD + 1 + seq_len, :] = u_ref[...]
    for s in range(1, SUBLANES):
        pad_ref[s, 0:n_pad - SUBLANES, :] = pad_ref[0, s:n_pad - SUBLANES + s, :]

    def window(first_row, tap):
        q, s = divmod(tap + 1, SUBLANES)
        return pad_ref[s, first_row + q * SUBLANES:first_row + q * SUBLANES + blk, :]

    bias = b_ref[...]
    lng = lng_ref[...]
    lnb = lnb_ref[...]
    for r in range(n_blk):
        if latent:
            acc_w = jnp.zeros((blk, half), F32)
            for t in range(CONV_K):
                acc_w += w_ref[t:t + 1, 0:half] * window(r * ROW_PITCH, t)
            acc_h = jnp.zeros((blk, half), F32)
            for r2 in range(n_blk):
                t = r2 - r + CONV_PAD
                acc_h += w_ref[t:t + 1, half:CONV_CH] * u_ref[r2 * blk:(r2 + 1) * blk, half:CONV_CH]
            y = jnp.concatenate([acc_w, acc_h], axis=-1) + bias
        else:
            acc = jnp.zeros((blk, CONV_CH), F32)
            for t in range(CONV_K):
                acc += w_ref[t:t + 1, :] * window(r * blk, t)
            y = acc + bias
        mu = jnp.mean(y, axis=-1, keepdims=True)
        yc = y - mu
        var = jnp.mean(yc * yc, axis=-1, keepdims=True)
        yn = (yc * lax.rsqrt(var + EPS)) * lng + lnb
        y_ref[r * blk:(r + 1) * blk, :] = (yn * jax.nn.sigmoid(yn)).astype(BF16)
    o_ref[...] = _dot(y_ref[...], pw_ref[...]).astype(o_ref.dtype)


def _conv(u, w, b, lng, lnb, pw, prev_out, *, seq_len, n_seq, first_block, latent):
    row = lambda i: (first_block + i, 0)
    const = lambda i: (0, 0)
    if latent:
        pad_shape = (SUBLANES, GRID_H * ROW_PITCH, CONV_CH // 2)
    else:
        pad_shape = (SUBLANES, seq_len + 2 * (CONV_PAD + 1), CONV_CH)
    return pl.pallas_call(
        functools.partial(_conv_kernel, latent=latent, seq_len=seq_len),
        grid=(n_seq,),
        in_specs=[
            pl.BlockSpec((seq_len, CONV_CH), row),
            pl.BlockSpec((CONV_K + 1, CONV_CH), const),
            pl.BlockSpec((1, CONV_CH), const), pl.BlockSpec((1, CONV_CH), const), pl.BlockSpec((1, CONV_CH), const),
            pl.BlockSpec((CONV_CH, CONV_CH), const),
            pl.BlockSpec(memory_space=pl.ANY),
        ],
        out_specs=pl.BlockSpec((seq_len, CONV_CH), row),
        out_shape=jax.ShapeDtypeStruct(prev_out.shape, prev_out.dtype),
        scratch_shapes=[pltpu.VMEM(pad_shape, F32), pltpu.VMEM((seq_len, CONV_CH), BF16)],
        input_output_aliases={6: 0},
        compiler_params=_params("parallel"),
        name="conv_lat" if latent else "conv_ctx",
    )(u, w, b, lng, lnb, pw, prev_out)


def _mid_kernel(mo_ref, uu_ref, xp_ref, xs_ref, mod_ref, g2_ref, wo1_ref, wo2_ref, wqt_ref, keys_ref,
                x1_ref, h2t_ref, st_ref):
    mod = mod_ref[0]
    gate1 = mod[:, 2 * D_MODEL:3 * D_MODEL]
    shift2 = mod[:, 3 * D_MODEL:4 * D_MODEL]
    scale2 = mod[:, 4 * D_MODEL:5 * D_MODEL]
    m = _dot(mo_ref[...], wo1_ref[...]) + _dot(uu_ref[...], wo2_ref[...])
    x1 = _token_tile(xp_ref, xs_ref, TOK_TILE) + gate1 * m
    x1_ref[...] = x1
    ms = jnp.mean(x1 * x1, axis=-1, keepdims=True)
    h2 = (x1 * lax.rsqrt(ms + EPS)) * g2_ref[...]
    h2 = h2 * (1.0 + scale2) + shift2
    h2t = h2.T.astype(BF16)
    h2t_ref[...] = h2t
    qt = _dot(wqt_ref[...], h2t)
    for hp in range(2 * PEER_HEADS):
        rows = slice(hp * PEER_NKEYS, (hp + 1) * PEER_NKEYS)
        st_ref[rows, :] = _dot(keys_ref[hp], qt[rows, :].astype(BF16))


def _mid(mo, uu, xp, xs, mod3, g2, wo1, wo2, wqt, keys):
    tm = TOK_TILE
    row = lambda i: (i, 0)
    col = lambda i: (0, i)
    const = lambda i: (0, 0)
    n_q = PEER_HEADS * PEER_DQ
    return pl.pallas_call(
        _mid_kernel,
        grid=(N_TOK // tm,),
        in_specs=[
            pl.BlockSpec((tm, GLA_DV), row), pl.BlockSpec((tm, CONV_CH), row), _ctx_spec(tm), _lat_spec(tm),
            pl.BlockSpec((1, 1, 6 * D_MODEL), lambda i: (_mod_row(i, tm), 0, 0)),
            pl.BlockSpec((1, D_MODEL), const),
            pl.BlockSpec((GLA_DV, D_MODEL), const), pl.BlockSpec((CONV_CH, D_MODEL), const),
            pl.BlockSpec((n_q, D_MODEL), const),
            pl.BlockSpec((2 * PEER_HEADS, PEER_NKEYS, PEER_DQ // 2), lambda i: (0, 0, 0)),
        ],
        out_specs=[pl.BlockSpec((tm, D_MODEL), row), pl.BlockSpec((D_MODEL, tm), col), pl.BlockSpec((n_q, tm), col)],
        out_shape=[jax.ShapeDtypeStruct((N_TOK, D_MODEL), F32), jax.ShapeDtypeStruct((D_MODEL, N_TOK), BF16),
                   jax.ShapeDtypeStruct((n_q, N_TOK), F32)],
        compiler_params=_params("parallel"),
        name="mid",
    )(mo, uu, xp, xs, mod3, g2, wo1, wo2, wqt, keys)


N_CAND_ROWS = 80


def _top16(s, key_iota, sv_ref, exact, want_rank):
    rank = jnp.full(s.shape, PEER_TOPK, jnp.int32) if want_rank else None
    for a in range(PEER_TOPK):
        m = jnp.max(s, axis=0, keepdims=True)
        hit = s == m
        if exact:
            first = jnp.min(jnp.where(hit, key_iota, PEER_NKEYS), axis=0, keepdims=True)
            hit = key_iota == first
        if want_rank:
            rank = jnp.where(hit, a, rank)
        s = jnp.where(hit, -jnp.inf, s)
        sv_ref[a:a + 1, :] = m
    return rank, s


def _count_true(mask):
    return jnp.sum(jnp.where(mask, 1.0, 0.0), axis=0, keepdims=True)


def _select_head(h, bad, st_ref, pa_ref, pb_ref, sv1_ref, sv2_ref, cand_ref, *, exact):
    key_iota = lax.broadcasted_iota(jnp.int32, (PEER_NKEYS, SEL_TILE), 0)
    crow = lax.broadcasted_iota(jnp.int32, (N_CAND_ROWS, SEL_TILE), 0)
    flat = jnp.where(crow < 16, crow * 16,
                     jnp.where(crow < 72, ((crow - 16) % 8) * 16 + (crow - 16) // 8 + 1, crow - 72 + 8))
    base = pl.multiple_of(h * 2 * PEER_NKEYS, 2 * PEER_NKEYS)
    s1 = st_ref[pl.ds(base, PEER_NKEYS), :]
    s2 = st_ref[pl.ds(base + PEER_NKEYS, PEER_NKEYS), :]
    r1, _ = _top16(s1, key_iota, sv1_ref, exact, True)
    r2, s2_left = _top16(s2, key_iota, sv2_ref, exact, exact)
    top1 = sv1_ref[0:1, :]
    top2 = sv2_ref[0:1, :]
    cand_ref[0:16, :] = sv1_ref[...] + top2
    for b in range(1, 8):
        cand_ref[8 + 8 * b:16 + 8 * b, :] = sv1_ref[0:8, :] + sv2_ref[b:b + 1, :]
    cand_ref[72:80, :] = top1 + sv2_ref[8:16, :]
    cand = cand_ref[...]
    cmax = top1 + top2
    sel = jnp.zeros(cand.shape, jnp.bool_)
    z = jnp.zeros((1, SEL_TILE), F32)
    for _ in range(PEER_TOPK):
        m = jnp.max(cand, axis=0, keepdims=True)
        hit = cand == m
        if exact:
            first = jnp.min(jnp.where(hit, flat, 256), axis=0, keepdims=True)
            hit = flat == first
        sel = sel | hit
        cand = jnp.where(hit, -jnp.inf, cand)
        z = z + jnp.exp(m - cmax)
    self32 = jnp.where(sel, 1.0, 0.0)
    counts = [jnp.sum(self32[0:16, :], axis=0, keepdims=True)]
    for b in range(1, 8):
        counts.append(jnp.sum(self32[8 + 8 * b:16 + 8 * b, :], axis=0, keepdims=True))
    for b in range(8, 16):
        counts.append(self32[72 + b - 8:72 + b - 7, :])
    n2 = jnp.zeros((PEER_NKEYS, SEL_TILE), F32)
    for b in range(PEER_TOPK):
        is_b = (r2 == b) if exact else (s2 == sv2_ref[b:b + 1, :])
        n2 = jnp.where(is_b, counts[b], n2)
    out = pl.multiple_of(h * SEL_ROWS_PER_HEAD, SEL_ROWS_PER_HEAD)
    pa_ref[pl.ds(out, PEER_NKEYS), :] = r1.astype(F32)
    pa_ref[pl.ds(out + PEER_NKEYS, PEER_NKEYS), :] = jnp.exp(s1 - top1)
    pb_ref[pl.ds(out, PEER_NKEYS), :] = n2.astype(BF16)
    pb_ref[pl.ds(out + PEER_NKEYS, PEER_NKEYS), :] = (jnp.exp(s2 - top2) / z).astype(BF16)
    if exact:
        return bad
    n_taken = _count_true(r1 < PEER_TOPK) + _count_true(s2_left == -jnp.inf) + jnp.sum(self32, axis=0, keepdims=True)
    return jnp.maximum(bad, jnp.where(n_taken == 3.0 * PEER_TOPK, 0.0, 1.0))


HEADS_PER_TRIP = 4


def _select_kernel(st_ref, pa_ref, pb_ref, sv1_ref, sv2_ref, cand_ref):
    def trip(exact):
        def body(t, bad):
            for k in range(HEADS_PER_TRIP):
                bad = _select_head(t * HEADS_PER_TRIP + k, bad, st_ref, pa_ref, pb_ref,
                                   sv1_ref.at[k], sv2_ref.at[k], cand_ref.at[k], exact=exact)
            return bad
        return body

    bad = lax.fori_loop(0, PEER_HEADS // HEADS_PER_TRIP, trip(False), jnp.zeros((1, SEL_TILE), F32))

    @pl.when(jnp.max(bad) > 0.0)
    def _():
        lax.fori_loop(0, PEER_HEADS // HEADS_PER_TRIP, trip(True), bad)


def _select(st):
    n_q = PEER_HEADS * PEER_DQ
    n_pe = PEER_HEADS * SEL_ROWS_PER_HEAD
    col = lambda i: (0, i)
    return pl.pallas_call(
        _select_kernel,
        grid=(N_TOK // SEL_TILE,),
        in_specs=[pl.BlockSpec((n_q, SEL_TILE), col)],
        out_specs=[pl.BlockSpec((n_pe, SEL_TILE), col), pl.BlockSpec((n_pe, SEL_TILE), col)],
        out_shape=[jax.ShapeDtypeStruct((n_pe, N_TOK), F32), jax.ShapeDtypeStruct((n_pe, N_TOK), BF16)],
        scratch_shapes=[pltpu.VMEM((HEADS_PER_TRIP, PEER_TOPK, SEL_TILE), F32),
                        pltpu.VMEM((HEADS_PER_TRIP, PEER_TOPK, SEL_TILE), F32),
                        pltpu.VMEM((HEADS_PER_TRIP, N_CAND_ROWS, SEL_TILE), F32)],
        compiler_params=_params("parallel"),
        name="select",
    )(st)


N_EXP_TILES = PEER_N // EXP_TILE
N_EXP_TOK_TILES = N_TOK // EXP_TOK_TILE
N_EXP_WORK = N_EXP_TOK_TILES * N_EXP_TILES
N_EXP_STEPS = N_EXP_WORK + 2


def _experts_kernel(h2t_ref, pa_ref, pb_ref, u_ref, vt_ref, x1_ref, mod_ref, gfin_ref, yp_ref, ys_ref,
                    acc_ref, a_ref, c_ref):
    s = pl.program_id(0)
    item_gate = jnp.clip(s - 1, 0, N_EXP_WORK - 1)
    e_gate = item_gate % N_EXP_TILES
    e_out = jnp.maximum(s - 2, 0) % N_EXP_TILES
    tile3 = (PEER_NKEYS // BF16_ROWS, BF16_ROWS, LANES)

    @pl.when(s == 0)
    def _():
        a_ref[...] = jnp.zeros_like(a_ref)
        c_ref[...] = jnp.zeros_like(c_ref)

    @pl.when((s == 0) | ((s >= 2) & (e_out == 0)))
    def _():
        acc_ref[...] = jnp.zeros_like(acc_ref)

    def token_group(grp):
        cols = slice(grp * EXP_TOK_GROUP, (grp + 1) * EXP_TOK_GROUP)
        acc_ref[:, cols] += _dot(vt_ref[...], c_ref[:, cols])
        for ib in range(EXP_TILE // PEER_NKEYS):
            i = e_gate * (EXP_TILE // PEER_NKEYS) + ib
            rows = slice(ib * PEER_NKEYS, (ib + 1) * PEER_NKEYS)
            rank_rows = [pa_ref[pl.ds(h * SEL_ROWS_PER_HEAD + i, 1), :] for h in range(PEER_HEADS)]
            e1_rows = [pa_ref[pl.ds(h * SEL_ROWS_PER_HEAD + PEER_NKEYS + i, 1), :] for h in range(PEER_HEADS)]
            for lc in range(EXP_TOK_GROUP // LANES):
                first = grp * EXP_TOK_GROUP + lc * LANES
                lanes = slice(first, first + LANES)
                a = a_ref[rows, lanes]
                act = (0.5 * a * (1.0 + lax.erf(a * (2.0 ** -0.5)))).astype(BF16).reshape(tile3)
                w = jnp.zeros(tile3, BF16)
                for h in range(PEER_HEADS):
                    base = h * SEL_ROWS_PER_HEAD
                    rank_i = jnp.broadcast_to(rank_rows[h][:, lanes], (BF16_ROWS, LANES)).astype(BF16)
                    e1_i = jnp.broadcast_to(e1_rows[h][:, lanes], (BF16_ROWS, LANES)).astype(BF16)
                    n2 = pb_ref[base:base + PEER_NKEYS, lanes].reshape(tile3)
                    e2 = pb_ref[base + PEER_NKEYS:base + 2 * PEER_NKEYS, lanes].reshape(tile3)
                    w = w + jnp.where(rank_i[None] < n2, e1_i[None] * e2, jnp.zeros((), BF16))
                c_ref[rows, lanes] = (act * w).reshape(PEER_NKEYS, LANES)
        a_ref[:, cols] = _dot(u_ref[...], h2t_ref[:, cols])

    for grp in range(EXP_TOK_TILE // EXP_TOK_GROUP):
        token_group(grp)

    @pl.when((s >= 2) & (e_out == N_EXP_TILES - 1))
    def _():
        mod = mod_ref[0]
        gate2 = mod[:, 5 * D_MODEL:6 * D_MODEL]
        y = x1_ref[...] + gate2 * acc_ref[...].T
        ms = jnp.mean(y * y, axis=-1, keepdims=True)
        y = (y * lax.rsqrt(ms + EPS)) * gfin_ref[...]
        is_ctx = jnp.maximum(s - 2, 0) // N_EXP_TILES < N_CTX_TOK // EXP_TOK_TILE

        @pl.when(is_ctx)
        def _():
            yp_ref[...] = y

        @pl.when(jnp.logical_not(is_ctx))
        def _():
            ys_ref[...] = y


def _experts(h2t, pa, pb, u_bf, vt_bf, x1, mod3, gfin):
    tm = EXP_TOK_TILE
    n_pe = PEER_HEADS * SEL_ROWS_PER_HEAD
    item = lambda s, lag: jnp.clip(s - lag, 0, N_EXP_WORK - 1)
    tok = lambda s, lag: item(s, lag) // N_EXP_TILES
    exp = lambda s, lag: item(s, lag) % N_EXP_TILES
    n_ctx_tiles = N_CTX_TOK // tm
    return pl.pallas_call(
        _experts_kernel,
        grid=(N_EXP_STEPS,),
        in_specs=[
            pl.BlockSpec((D_MODEL, tm), lambda s: (0, tok(s, 0))),
            pl.BlockSpec((n_pe, tm), lambda s: (0, tok(s, 1))),
            pl.BlockSpec((n_pe, tm), lambda s: (0, tok(s, 1))),
            pl.BlockSpec((EXP_TILE, D_MODEL), lambda s: (exp(s, 0), 0)),
            pl.BlockSpec((D_MODEL, EXP_TILE), lambda s: (0, exp(s, 2))),
            pl.BlockSpec((tm, D_MODEL), lambda s: (tok(s, 2), 0)),
            pl.BlockSpec((1, 1, 6 * D_MODEL), lambda s: (_mod_row(tok(s, 2), tm), 0, 0)),
            pl.BlockSpec((1, D_MODEL), lambda s: (0, 0)),
        ],
        out_specs=[pl.BlockSpec((tm, D_MODEL), lambda s: (jnp.minimum(tok(s, 2), n_ctx_tiles - 1), 0)),
                   pl.BlockSpec((tm, D_MODEL), lambda s: (jnp.maximum(tok(s, 2) - n_ctx_tiles, 0), 0))],
        out_shape=[jax.ShapeDtypeStruct((N_CTX_TOK, D_MODEL), F32), jax.ShapeDtypeStruct((N_LAT_TOK, D_MODEL), F32)],
        scratch_shapes=[pltpu.VMEM((D_MODEL, tm), F32), pltpu.VMEM((EXP_TILE, tm), F32),
                        pltpu.VMEM((EXP_TILE, tm), BF16)],
        compiler_params=_params("arbitrary"),
        name="experts",
    )(h2t, pa, pb, u_bf, vt_bf, x1, mod3, gfin)


def kernel(x_prompt, x_sample, c, state_gla_fwd, state_gla_bwd, c_ctx, norm1_g, w_mod, b_mod, w_in, w_af_up, b_af,
           w_ab_up, b_ab, gla_norm_g, conv_dw_w, conv_dw_b, conv_ln_g, conv_ln_b, conv_pw2, w_out, norm2_g, peer_wq,
           peer_subkeys, peer_u, peer_v, final_norm_g):
    depth = w_in.shape[0]
    assert depth == 1
    l = 0
    xp = x_prompt.reshape(N_CTX_TOK, D_MODEL)
    xs = x_sample.reshape(N_LAT_TOK, D_MODEL)

    c_all = jnp.zeros((N_MOD_PAD, D_MODEL), F32).at[0].set(c_ctx).at[1:N_MOD_ROWS].set(c)
    w = w_in[l]
    w_main = jnp.concatenate([w[:, 0:1536], w[:, 1568:2592]], axis=1).astype(BF16)
    w_ab = jnp.zeros((D_MODEL, LANES), F32).at[:, 0:2 * GLA_RANK].set(w[:, 1536:1568]).astype(BF16)
    w_up = jnp.zeros((LANES, 2 * GLA_DK), F32)
    w_up = w_up.at[0:GLA_RANK, 0:GLA_DK].set(w_af_up[l]).at[GLA_RANK:2 * GLA_RANK, GLA_DK:].set(w_ab_up[l]).astype(BF16)
    b_up = jnp.concatenate([b_af[l], b_ab[l]])[None, :]
    conv_w = jnp.zeros((CONV_K + 1, CONV_CH), F32).at[0:CONV_K].set(conv_dw_w[l])
    wo1 = w_out[l][0:GLA_DV].astype(BF16)
    wo2 = w_out[l][GLA_DV:].astype(BF16)
    wqt = peer_wq[l].T.astype(BF16)
    keys = peer_subkeys[l].reshape(2 * PEER_HEADS, PEER_NKEYS, PEER_DQ // 2).astype(BF16)
    u_bf = peer_u[l].astype(BF16)
    vt_bf = peer_v[l].T.astype(BF16)

    mod = _modulation(c_all, w_mod[l], b_mod[l][None, :])
    mod3 = mod[:, None, :]

    q, k, v, gf, gb, sg, u = _mixin(xp, xs, mod3, norm1_g[l][None, :], w_main, w_ab, w_up, b_up)

    gn = gla_norm_g[l][None, :]
    mo0 = jnp.zeros((N_TOK, GLA_DV), BF16)
    mo1, new_f, new_b = _gla(q, k, v, gf, gb, sg, gn, mo0, seq_len=CTX_LEN, n_seq=N_CTX_SEQ, first_block=0,
                             init_states=None, emit_state=True)
    (mo,) = _gla(q, k, v, gf, gb, sg, gn, mo1, seq_len=LAT_LEN, n_seq=N_LAT_SEQ,
                 first_block=N_CTX_TOK // LAT_LEN, init_states=(state_gla_fwd[:, l], state_gla_bwd[:, l]),
                 emit_state=False)

    conv_args = (conv_w, conv_dw_b[l][None, :], conv_ln_g[l][None, :], conv_ln_b[l][None, :], conv_pw2[l].astype(BF16))
    uu0 = jnp.zeros((N_TOK, CONV_CH), BF16)
    uu1 = _conv(u, *conv_args, uu0, seq_len=CTX_LEN, n_seq=N_CTX_SEQ, first_block=0, latent=False)
    uu = _conv(u, *conv_args, uu1, seq_len=LAT_LEN, n_seq=N_LAT_SEQ, first_block=N_CTX_TOK // LAT_LEN, latent=True)

    x1, h2t, st = _mid(mo, uu, xp, xs, mod3, norm2_g[l][None, :], wo1, wo2, wqt, keys)
    pa, pb = _select(st)
    yp, ys = _experts(h2t, pa, pb, u_bf, vt_bf, x1, mod3, final_norm_g[None, :])

    y_prompt = yp.reshape(N_CTX_SEQ, CTX_LEN, D_MODEL)
    y_sample = ys.reshape(N_LAT_SEQ, LAT_LEN, D_MODEL)
    return (y_prompt, y_sample, new_f[:, None], new_b[:, None])
```

```python
import functools

import jax
import jax.numpy as jnp
from jax import lax
from jax.experimental import pallas as pl
from jax.experimental.pallas import tpu as pltpu

F32 = jnp.float32
BF16 = jnp.bfloat16

D_MODEL = 1024
N_CTX_SEQ = 32
CTX_LEN = 256
N_LAT_SEQ = 8
LAT_LEN = 1024
N_CTX_TOK = N_CTX_SEQ * CTX_LEN
N_LAT_TOK = N_LAT_SEQ * LAT_LEN
N_TOK = N_CTX_TOK + N_LAT_TOK
GRID_W = 64
GRID_H = LAT_LEN // GRID_W
GLA_HEADS = 4
GLA_DK = 256
GLA_DV = 512
GLA_HK = 64
GLA_HV = 128
GLA_RANK = 16
GLA_NORMALIZER = 16.0
CHUNK = 64
CUM_ROWS = 256
CONV_CH = 512
CONV_K = 31
CONV_PAD = CONV_K // 2
PEER_HEADS = 8
PEER_NKEYS = 128
PEER_N = PEER_NKEYS * PEER_NKEYS
PEER_DQ = 256
PEER_TOPK = 16
EPS = 1e-6
GELU_IN_SCALE = 2.0 ** -0.5
N_MOD_ROWS = 1 + N_LAT_SEQ
N_MOD_PAD = 16

LANES = 128
SUBLANES = 8
VMEM_LIMIT_BYTES = 56 * 1024 * 1024

TOK_TILE = 512
SEL_TILE = LANES
EXP_TOK_TILE = 512
EXP_TOK_GROUP = 256
EXP_TILE = 1024
SEL_ROWS_PER_HEAD = 2 * PEER_NKEYS
BF16_ROWS = 16


def _params(*semantics):
    return pltpu.CompilerParams(dimension_semantics=semantics, vmem_limit_bytes=VMEM_LIMIT_BYTES)


def _mod_row(tile, tile_rows):
    first = tile * tile_rows
    return jnp.where(first < N_CTX_TOK, 0, 1 + (first - N_CTX_TOK) // LAT_LEN)


def _ctx_spec(tile_rows):
    n_ctx_tiles = N_CTX_TOK // tile_rows
    return pl.BlockSpec((tile_rows, D_MODEL), lambda i: (jnp.minimum(i, n_ctx_tiles - 1), 0))


def _lat_spec(tile_rows):
    n_ctx_tiles = N_CTX_TOK // tile_rows
    return pl.BlockSpec((tile_rows, D_MODEL), lambda i: (jnp.maximum(i - n_ctx_tiles, 0), 0))


def _token_tile(xp_ref, xs_ref, tile_rows):
    is_ctx = pl.program_id(0) < N_CTX_TOK // tile_rows
    return jnp.where(is_ctx, xp_ref[...], xs_ref[...])


def _split_bf16(a):
    hi = a.astype(BF16)
    lo = (a - hi.astype(F32)).astype(BF16)
    return hi, lo


def _dot(a, b):
    return jnp.dot(a, b, preferred_element_type=F32)


def _dot_nt(a, b):
    return lax.dot_general(a, b, (((1,), (1,)), ((), ())), preferred_element_type=F32)


def _dot_tn(a, b):
    return lax.dot_general(a, b, (((0,), (0,)), ((), ())), preferred_element_type=F32)


def _mod_kernel(c_ref, w_ref, b_ref, o_ref):
    c = c_ref[...]
    a = c * jax.nn.sigmoid(c)
    a_hi, a_lo = _split_bf16(a)
    w_hi, w_lo = _split_bf16(w_ref[...])
    o_ref[...] = _dot(a_hi, w_hi) + _dot(a_hi, w_lo) + _dot(a_lo, w_hi) + b_ref[...]


def _modulation(c_all, w_mod, b_mod):
    n_out = w_mod.shape[1]
    tile = 1024
    return pl.pallas_call(
        _mod_kernel,
        grid=(n_out // tile,),
        in_specs=[
            pl.BlockSpec((N_MOD_PAD, D_MODEL), lambda j: (0, 0)),
            pl.BlockSpec((D_MODEL, tile), lambda j: (0, j)),
            pl.BlockSpec((1, tile), lambda j: (0, j)),
        ],
        out_specs=pl.BlockSpec((N_MOD_PAD, tile), lambda j: (0, j)),
        out_shape=jax.ShapeDtypeStruct((N_MOD_PAD, n_out), F32),
        compiler_params=_params("parallel"),
        name="mod",
    )(c_all, w_mod, b_mod)


N_MAIN = 2 * GLA_DK + 2 * GLA_DV + 2 * CONV_CH


def _mixin_kernel(xp_ref, xs_ref, mod_ref, g1_ref, wm_ref, wab_ref, wup_ref, bup_ref,
                  q_ref, k_ref, v_ref, gf_ref, gb_ref, sg_ref, u_ref):
    x = _token_tile(xp_ref, xs_ref, TOK_TILE)
    mod = mod_ref[0]
    shift1 = mod[:, 0:D_MODEL]
    scale1 = mod[:, D_MODEL:2 * D_MODEL]
    ms = jnp.mean(x * x, axis=-1, keepdims=True)
    h = (x * lax.rsqrt(ms + EPS)) * g1_ref[...]
    h = h * (1.0 + scale1) + shift1
    hb = h.astype(BF16)
    p = _dot(hb, wm_ref[...])
    q_ref[...] = p[:, 0:256] * (GLA_HK ** -0.5)
    k_ref[...] = p[:, 256:512]
    v_ref[...] = p[:, 512:1024]
    g = p[:, 1024:1536]
    sg_ref[...] = g * jax.nn.sigmoid(g)
    ca = p[:, 1536:2048]
    cb = p[:, 2048:2560]
    u_ref[...] = ca * jax.nn.sigmoid(cb)
    a = _dot(hb, wab_ref[...])
    z = _dot(a.astype(BF16), wup_ref[...]) + bup_ref[...]
    ls = jnp.minimum(z, 0.0) - jnp.log(1.0 + jnp.exp(-jnp.abs(z)))
    gl = ls * (1.0 / GLA_NORMALIZER)
    gf_ref[...] = gl[:, 0:GLA_DK]
    gb_ref[...] = gl[:, GLA_DK:2 * GLA_DK]


def _mixin(xp, xs, mod3, g1, w_main, w_ab, w_up, b_up):
    tm = TOK_TILE
    row = lambda i: (i, 0)
    const = lambda i: (0, 0)
    outs = [(GLA_DK, F32), (GLA_DK, F32), (GLA_DV, F32), (GLA_DK, F32), (GLA_DK, F32),
            (GLA_DV, F32), (CONV_CH, F32)]
    return pl.pallas_call(
        _mixin_kernel,
        grid=(N_TOK // tm,),
        in_specs=[
            _ctx_spec(tm), _lat_spec(tm),
            pl.BlockSpec((1, 1, 6 * D_MODEL), lambda i: (_mod_row(i, tm), 0, 0)),
            pl.BlockSpec((1, D_MODEL), const),
            pl.BlockSpec((D_MODEL, N_MAIN), const),
            pl.BlockSpec((D_MODEL, LANES), const),
            pl.BlockSpec((LANES, 2 * GLA_DK), const),
            pl.BlockSpec((1, 2 * GLA_DK), const),
        ],
        out_specs=[pl.BlockSpec((tm, n), row) for n, _ in outs],
        out_shape=[jax.ShapeDtypeStruct((N_TOK, n), dt) for n, dt in outs],
        compiler_params=_params("parallel"),
        name="mixin",
    )(xp, xs, mod3, g1, w_main, w_ab, w_up, b_up)


def _gla_kernel(*refs, seq_len, has_init, emit_state):
    q_ref, k_ref, v_ref, gf_ref, gb_ref, sg_ref, gn_ref = refs[:7]
    pos = 7
    if has_init:
        s0f_ref, s0b_ref = refs[pos:pos + 2]
        pos += 2
    pos += 1
    o_ref = refs[pos]
    pos += 1
    if emit_state:
        sf_ref, sb_ref = refs[pos:pos + 2]
        pos += 2
    oacc_ref, stf_ref, stb_ref, bf_ref, bb_ref = refs[pos:pos + 5]

    n_chunks = seq_len // CHUNK
    lane = lax.broadcasted_iota(jnp.int32, (1, GLA_DK), 1)
    head_mask = [(lane >= h * GLA_HK) & (lane < (h + 1) * GLA_HK) for h in range(GLA_HEADS)]
    srow = lax.broadcasted_iota(jnp.int32, (GLA_HEADS * CHUNK, CHUNK), 0) % CHUNK
    scol = lax.broadcasted_iota(jnp.int32, (GLA_HEADS * CHUNK, CHUNK), 1)
    ones_t = jnp.ones((CHUNK, GLA_HV), BF16)

    gi = lax.broadcasted_iota(jnp.int32, (CUM_ROWS, CUM_ROWS), 0)
    gj = lax.broadcasted_iota(jnp.int32, (CUM_ROWS, CUM_ROWS), 1)
    same_chunk = (gi // CHUNK) == (gj // CHUNK)
    tri_f = (same_chunk & (gi >= gj)).astype(BF16)
    tri_b = (same_chunk & (gi <= gj)).astype(BF16)
    for r in range(seq_len // CUM_ROWS):
        grp = slice(r * CUM_ROWS, (r + 1) * CUM_ROWS)
        for g_ref, b_ref, tri in ((gf_ref, bf_ref, tri_f), (gb_ref, bb_ref, tri_b)):
            g_hi, g_lo = _split_bf16(g_ref[grp, :])
            b_ref[grp, :] = _dot(tri, g_hi) + _dot(tri, g_lo)

    def chunk(c, g_ref, b_ref, st_ref, forward):
        if forward:
            causal = srow >= scol
            ref_row, last_row = CHUNK // 2 - 1, CHUNK - 1
        else:
            causal = srow <= scol
            ref_row, last_row = CHUNK // 2, 0
        r0 = pl.multiple_of(c * CHUNK, CHUNK)
        rows = pl.ds(r0, CHUNK)
        g = g_ref[rows, :]
        q = q_ref[rows, :]
        k = k_ref[rows, :]
        vb = v_ref[rows, :].astype(BF16)
        g_hi, g_lo = _split_bf16(g)
        b = b_ref[rows, :]
        b_mid = b[ref_row:ref_row + 1, :]
        b_end = b[last_row:last_row + 1, :]
        qt = q * jnp.exp(b - b_mid)
        kt = (k * jnp.exp(b_mid - b)).astype(BF16)
        qe = q * jnp.exp(b)
        kd = (k * jnp.exp(b_end - b)).astype(BF16)
        q_stack = jnp.concatenate([jnp.where(m, qt, 0.0) for m in head_mask], axis=0).astype(BF16)
        qe_stack = jnp.concatenate([jnp.where(m, qe, 0.0) for m in head_mask], axis=0).astype(BF16)
        scores = jnp.where(causal, _dot_nt(q_stack, kt), 0.0)
        o_intra = _dot(scores.astype(BF16), vb)
        state = st_ref[...]
        o_inter = _dot(qe_stack, state.astype(BF16))
        upd = _dot_tn(kd, vb)
        dec = jnp.exp(_dot_tn(g_hi, ones_t) + _dot_tn(g_lo, ones_t))
        for h in range(GLA_HEADS):
            hr = slice(h * CHUNK, (h + 1) * CHUNK)
            hv = slice(h * GLA_HV, (h + 1) * GLA_HV)
            oacc_ref[rows, hv] += o_intra[hr, hv] + o_inter[hr, :]
            st_ref[hr, :] = dec[hr, :] * state[hr, :] + upd[hr, hv]

    oacc_ref[...] = jnp.zeros_like(oacc_ref)
    if has_init:
        stf_ref[...] = s0f_ref[0].reshape(GLA_HEADS * GLA_HK, GLA_HV)
        stb_ref[...] = s0b_ref[0].reshape(GLA_HEADS * GLA_HK, GLA_HV)
    else:
        stf_ref[...] = jnp.zeros_like(stf_ref)
        stb_ref[...] = jnp.zeros_like(stb_ref)

    def step(it, carry):
        chunk(it, gf_ref, bf_ref, stf_ref, True)
        chunk(n_chunks - 1 - it, gb_ref, bb_ref, stb_ref, False)
        return carry

    lax.fori_loop(0, n_chunks, step, 0)
    if emit_state:
        sf_ref[0] = stf_ref[...].reshape(GLA_HEADS, GLA_HK, GLA_HV)
        sb_ref[0] = stb_ref[...].reshape(GLA_HEADS, GLA_HK, GLA_HV)

    o = oacc_ref[...]
    gn = gn_ref[...]
    sg = sg_ref[...]
    for h in range(GLA_HEADS):
        hv = slice(h * GLA_HV, (h + 1) * GLA_HV)
        oh = o[:, hv]
        ms = jnp.mean(oh * oh, axis=-1, keepdims=True)
        o_ref[:, hv] = ((oh * lax.rsqrt(ms + EPS)) * gn[:, hv] * sg[:, hv]).astype(o_ref.dtype)


def _gla(q, k, v, gf, gb, sg, gn, prev_out, *, seq_len, n_seq, first_block, init_states, emit_state):
    row = lambda i: (first_block + i, 0)
    const = lambda i: (0, 0)
    st_spec = pl.BlockSpec((1, GLA_HEADS, GLA_HK, GLA_HV), lambda i: (i, 0, 0, 0))
    in_specs = [
        pl.BlockSpec((seq_len, GLA_DK), row), pl.BlockSpec((seq_len, GLA_DK), row),
        pl.BlockSpec((seq_len, GLA_DV), row), pl.BlockSpec((seq_len, GLA_DK), row),
        pl.BlockSpec((seq_len, GLA_DK), row), pl.BlockSpec((seq_len, GLA_DV), row),
        pl.BlockSpec((1, GLA_DV), const),
    ]
    args = [q, k, v, gf, gb, sg, gn]
    if init_states is not None:
        in_specs += [st_spec, st_spec]
        args += list(init_states)
    in_specs.append(pl.BlockSpec(memory_space=pl.ANY))
    args.append(prev_out)
    out_specs = [pl.BlockSpec((seq_len, GLA_DV), row)]
    out_shape = [jax.ShapeDtypeStruct(prev_out.shape, prev_out.dtype)]
    if emit_state:
        out_specs += [st_spec, st_spec]
        out_shape += [jax.ShapeDtypeStruct((n_seq, GLA_HEADS, GLA_HK, GLA_HV), F32)] * 2
    return pl.pallas_call(
        functools.partial(_gla_kernel, seq_len=seq_len, has_init=init_states is not None, emit_state=emit_state),
        grid=(n_seq,),
        in_specs=in_specs,
        out_specs=out_specs,
        out_shape=out_shape,
        scratch_shapes=[pltpu.VMEM((seq_len, GLA_DV), F32), pltpu.VMEM((GLA_HEADS * GLA_HK, GLA_HV), F32),
                        pltpu.VMEM((GLA_HEADS * GLA_HK, GLA_HV), F32),
                        pltpu.VMEM((seq_len, GLA_DK), F32), pltpu.VMEM((seq_len, GLA_DK), F32)],
        input_output_aliases={len(args) - 1: 0},
        compiler_params=_params("parallel"),
        name="gla_lat" if init_states is not None else "gla_ctx",
    )(*args)


ROW_PITCH = GRID_W + 2 * (CONV_PAD + 1)


def _conv_kernel(u_ref, w_ref, b_ref, lng_ref, lnb_ref, pw_ref, prev_ref, o_ref, pad_ref, y_ref, *, latent, seq_len):
    del prev_ref
    half = CONV_CH // 2
    blk = GRID_W
    n_blk = seq_len // blk
    n_pad = pad_ref.shape[1]
    pad_ref[0] = jnp.zeros(pad_ref.shape[1:], F32)
    if latent:
        for r in range(n_blk):
            pad_ref[0, r * ROW_PITCH + CONV_PAD + 1:r * ROW_PITCH + CONV_PAD + 1 + blk, :] = u_ref[r * blk:(r + 1) * blk, 0:half]
    else:
        pad_ref[0, CONV_PAD + 1:CONV_PAD + 1 + seq_len, :] = u_ref[...]
    for s in range(1, SUBLANES):
        pad_ref[s, 0:n_pad - SUBLANES, :] = pad_ref[0, s:n_pad - SUBLANES + s, :]

    def window(first_row, tap):
        q, s = divmod(tap + 1, SUBLANES)
        return pad_ref[s, first_row + q * SUBLANES:first_row + q * SUBLANES + blk, :]

    bias = b_ref[...]
    lng = lng_ref[...]
    lnb = lnb_ref[...]
    for r in range(n_blk):
        if latent:
            acc_w = jnp.zeros((blk, half), F32)
            for t in range(CONV_K):
                acc_w += w_ref[t:t + 1, 0:half] * window(r * ROW_PITCH, t)
            acc_h = jnp.zeros((blk, half), F32)
            for r2 in range(n_blk):
                t = r2 - r + CONV_PAD
                acc_h += w_ref[t:t + 1, half:CONV_CH] * u_ref[r2 * blk:(r2 + 1) * blk, half:CONV_CH]
            y = jnp.concatenate([acc_w, acc_h], axis=-1) + bias
        else:
            acc = jnp.zeros((blk, CONV_CH), F32)
            for t in range(CONV_K):
                acc += w_ref[t:t + 1, :] * window(r * blk, t)
            y = acc + bias
        mu = jnp.mean(y, axis=-1, keepdims=True)
        yc = y - mu
        var = jnp.mean(yc * yc, axis=-1, keepdims=True)
        yn = (yc * lax.rsqrt(var + EPS)) * lng + lnb
        y_ref[r * blk:(r + 1) * blk, :] = (yn * jax.nn.sigmoid(yn)).astype(BF16)
    o_ref[...] = _dot(y_ref[...], pw_ref[...]).astype(o_ref.dtype)


def _conv(u, w, b, lng, lnb, pw, prev_out, *, seq_len, n_seq, first_block, latent):
    row = lambda i: (first_block + i, 0)
    const = lambda i: (0, 0)
    if latent:
        pad_shape = (SUBLANES, GRID_H * ROW_PITCH, CONV_CH // 2)
    else:
        pad_shape = (SUBLANES, seq_len + 2 * (CONV_PAD + 1), CONV_CH)
    return pl.pallas_call(
        functools.partial(_conv_kernel, latent=latent, seq_len=seq_len),
        grid=(n_seq,),
        in_specs=[
            pl.BlockSpec((seq_len, CONV_CH), row),
            pl.BlockSpec((CONV_K + 1, CONV_CH), const),
            pl.BlockSpec((1, CONV_CH), const), pl.BlockSpec((1, CONV_CH), const), pl.BlockSpec((1, CONV_CH), const),
            pl.BlockSpec((CONV_CH, CONV_CH), const),
            pl.BlockSpec(memory_space=pl.ANY),
        ],
        out_specs=pl.BlockSpec((seq_len, CONV_CH), row),
        out_shape=jax.ShapeDtypeStruct(prev_out.shape, prev_out.dtype),
        scratch_shapes=[pltpu.VMEM(pad_shape, F32), pltpu.VMEM((seq_len, CONV_CH), BF16)],
        input_output_aliases={6: 0},
        compiler_params=_params("parallel"),
        name="conv_lat" if latent else "conv_ctx",
    )(u, w, b, lng, lnb, pw, prev_out)


def _mid_kernel(mo_ref, uu_ref, xp_ref, xs_ref, mod_ref, g2_ref, wo1_ref, wo2_ref, wqt_ref, keys_ref,
                x1_ref, h2t_ref, st_ref):
    mod = mod_ref[0]
    gate1 = mod[:, 2 * D_MODEL:3 * D_MODEL]
    shift2 = mod[:, 3 * D_MODEL:4 * D_MODEL]
    scale2 = mod[:, 4 * D_MODEL:5 * D_MODEL]
    m = _dot(mo_ref[...], wo1_ref[...]) + _dot(uu_ref[...], wo2_ref[...])
    x1 = _token_tile(xp_ref, xs_ref, TOK_TILE) + gate1 * m
    x1_ref[...] = x1
    ms = jnp.mean(x1 * x1, axis=-1, keepdims=True)
    h2 = (x1 * lax.rsqrt(ms + EPS)) * g2_ref[...]
    h2 = h2 * (1.0 + scale2) + shift2
    h2t_f32 = h2.T
    h2t = h2t_f32.astype(BF16)
    h2t_ref[...] = (h2t_f32 * GELU_IN_SCALE).astype(BF16)
    qt = _dot(wqt_ref[...], h2t)
    for hp in range(2 * PEER_HEADS):
        rows = slice(hp * PEER_NKEYS, (hp + 1) * PEER_NKEYS)
        st_ref[rows, :] = _dot(keys_ref[hp], qt[rows, :].astype(BF16))


def _mid(mo, uu, xp, xs, mod3, g2, wo1, wo2, wqt, keys):
    tm = TOK_TILE
    row = lambda i: (i, 0)
    col = lambda i: (0, i)
    const = lambda i: (0, 0)
    n_q = PEER_HEADS * PEER_DQ
    return pl.pallas_call(
        _mid_kernel,
        grid=(N_TOK // tm,),
        in_specs=[
            pl.BlockSpec((tm, GLA_DV), row), pl.BlockSpec((tm, CONV_CH), row), _ctx_spec(tm), _lat_spec(tm),
            pl.BlockSpec((1, 1, 6 * D_MODEL), lambda i: (_mod_row(i, tm), 0, 0)),
            pl.BlockSpec((1, D_MODEL), const),
            pl.BlockSpec((GLA_DV, D_MODEL), const), pl.BlockSpec((CONV_CH, D_MODEL), const),
            pl.BlockSpec((n_q, D_MODEL), const),
            pl.BlockSpec((2 * PEER_HEADS, PEER_NKEYS, PEER_DQ // 2), lambda i: (0, 0, 0)),
        ],
        out_specs=[pl.BlockSpec((tm, D_MODEL), row), pl.BlockSpec((D_MODEL, tm), col), pl.BlockSpec((n_q, tm), col)],
        out_shape=[jax.ShapeDtypeStruct((N_TOK, D_MODEL), F32), jax.ShapeDtypeStruct((D_MODEL, N_TOK), BF16),
                   jax.ShapeDtypeStruct((n_q, N_TOK), F32)],
        compiler_params=_params("parallel"),
        name="mid",
    )(mo, uu, xp, xs, mod3, g2, wo1, wo2, wqt, keys)


N_CAND_ROWS = 80


def _top16(s, key_iota, sv_ref, exact, want_rank):
    rank = jnp.full(s.shape, PEER_TOPK, jnp.int32) if want_rank else None
    for a in range(PEER_TOPK):
        m = jnp.max(s, axis=0, keepdims=True)
        hit = s == m
        if exact:
            first = jnp.min(jnp.where(hit, key_iota, PEER_NKEYS), axis=0, keepdims=True)
            hit = key_iota == first
        if want_rank:
            rank = jnp.where(hit, a, rank)
        s = jnp.where(hit, -jnp.inf, s)
        sv_ref[a:a + 1, :] = m
    return rank, s


def _count_true(mask):
    return jnp.sum(jnp.where(mask, 1.0, 0.0), axis=0, keepdims=True)


def _select_head(h, bad, st_ref, pa_ref, pb_ref, sv1_ref, sv2_ref, cand_ref, *, exact):
    key_iota = lax.broadcasted_iota(jnp.int32, (PEER_NKEYS, SEL_TILE), 0)
    crow = lax.broadcasted_iota(jnp.int32, (N_CAND_ROWS, SEL_TILE), 0)
    flat = jnp.where(crow < 16, crow * 16,
                     jnp.where(crow < 72, ((crow - 16) % 8) * 16 + (crow - 16) // 8 + 1, crow - 72 + 8))
    base = pl.multiple_of(h * 2 * PEER_NKEYS, 2 * PEER_NKEYS)
    s1 = st_ref[pl.ds(base, PEER_NKEYS), :]
    s2 = st_ref[pl.ds(base + PEER_NKEYS, PEER_NKEYS), :]
    r1, _ = _top16(s1, key_iota, sv1_ref, exact, True)
    r2, s2_left = _top16(s2, key_iota, sv2_ref, exact, exact)
    top1 = sv1_ref[0:1, :]
    top2 = sv2_ref[0:1, :]
    cand_ref[0:16, :] = sv1_ref[...] + top2
    for b in range(1, 8):
        cand_ref[8 + 8 * b:16 + 8 * b, :] = sv1_ref[0:8, :] + sv2_ref[b:b + 1, :]
    cand_ref[72:80, :] = top1 + sv2_ref[8:16, :]
    cand = cand_ref[...]
    cmax = top1 + top2
    sel = jnp.zeros(cand.shape, jnp.bool_)
    z = jnp.zeros((1, SEL_TILE), F32)
    for _ in range(PEER_TOPK):
        m = jnp.max(cand, axis=0, keepdims=True)
        hit = cand == m
        if exact:
            first = jnp.min(jnp.where(hit, flat, 256), axis=0, keepdims=True)
            hit = flat == first
        sel = sel | hit
        cand = jnp.where(hit, -jnp.inf, cand)
        z = z + jnp.exp(m - cmax)
    self32 = jnp.where(sel, 1.0, 0.0)
    counts = [jnp.sum(self32[0:16, :], axis=0, keepdims=True)]
    for b in range(1, 8):
        counts.append(jnp.sum(self32[8 + 8 * b:16 + 8 * b, :], axis=0, keepdims=True))
    for b in range(8, 16):
        counts.append(self32[72 + b - 8:72 + b - 7, :])
    n2 = jnp.zeros((PEER_NKEYS, SEL_TILE), F32)
    for b in range(PEER_TOPK):
        is_b = (r2 == b) if exact else (s2 == sv2_ref[b:b + 1, :])
        n2 = jnp.where(is_b, counts[b], n2)
    out = pl.multiple_of(h * SEL_ROWS_PER_HEAD, SEL_ROWS_PER_HEAD)
    pa_ref[pl.ds(out, PEER_NKEYS), :] = r1.astype(F32)
    pa_ref[pl.ds(out + PEER_NKEYS, PEER_NKEYS), :] = jnp.exp(s1 - top1)
    pb_ref[pl.ds(out, PEER_NKEYS), :] = n2.astype(BF16)
    pb_ref[pl.ds(out + PEER_NKEYS, PEER_NKEYS), :] = (jnp.exp(s2 - top2) * (GELU_IN_SCALE / z)).astype(BF16)
    if exact:
        return bad
    n_taken = _count_true(r1 < PEER_TOPK) + _count_true(s2_left == -jnp.inf) + jnp.sum(self32, axis=0, keepdims=True)
    return jnp.maximum(bad, jnp.where(n_taken == 3.0 * PEER_TOPK, 0.0, 1.0))


HEADS_PER_TRIP = 4


def _select_kernel(st_ref, pa_ref, pb_ref, sv1_ref, sv2_ref, cand_ref):
    def trip(exact):
        def body(t, bad):
            for k in range(HEADS_PER_TRIP):
                bad = _select_head(t * HEADS_PER_TRIP + k, bad, st_ref, pa_ref, pb_ref,
                                   sv1_ref.at[k], sv2_ref.at[k], cand_ref.at[k], exact=exact)
            return bad
        return body

    bad = lax.fori_loop(0, PEER_HEADS // HEADS_PER_TRIP, trip(False), jnp.zeros((1, SEL_TILE), F32))

    @pl.when(jnp.max(bad) > 0.0)
    def _():
        lax.fori_loop(0, PEER_HEADS // HEADS_PER_TRIP, trip(True), bad)


def _select(st):
    n_q = PEER_HEADS * PEER_DQ
    n_pe = PEER_HEADS * SEL_ROWS_PER_HEAD
    col = lambda i: (0, i)
    return pl.pallas_call(
        _select_kernel,
        grid=(N_TOK // SEL_TILE,),
        in_specs=[pl.BlockSpec((n_q, SEL_TILE), col)],
        out_specs=[pl.BlockSpec((n_pe, SEL_TILE), col), pl.BlockSpec((n_pe, SEL_TILE), col)],
        out_shape=[jax.ShapeDtypeStruct((n_pe, N_TOK), F32), jax.ShapeDtypeStruct((n_pe, N_TOK), BF16)],
        scratch_shapes=[pltpu.VMEM((HEADS_PER_TRIP, PEER_TOPK, SEL_TILE), F32),
                        pltpu.VMEM((HEADS_PER_TRIP, PEER_TOPK, SEL_TILE), F32),
                        pltpu.VMEM((HEADS_PER_TRIP, N_CAND_ROWS, SEL_TILE), F32)],
        compiler_params=_params("parallel"),
        name="select",
    )(st)


N_EXP_TILES = PEER_N // EXP_TILE
N_EXP_TOK_TILES = N_TOK // EXP_TOK_TILE
N_EXP_WORK = N_EXP_TOK_TILES * N_EXP_TILES
N_EXP_STEPS = N_EXP_WORK + 2


def _experts_kernel(h2t_ref, pa_ref, pb_ref, u_ref, vt_ref, x1_ref, mod_ref, gfin_ref, yp_ref, ys_ref,
                    acc_ref, a_ref, c_ref):
    s = pl.program_id(0)
    item_gate = jnp.clip(s - 1, 0, N_EXP_WORK - 1)
    e_gate = item_gate % N_EXP_TILES
    e_out = jnp.maximum(s - 2, 0) % N_EXP_TILES
    tile3 = (PEER_NKEYS // BF16_ROWS, BF16_ROWS, LANES)

    @pl.when(s == 0)
    def _():
        a_ref[...] = jnp.zeros_like(a_ref)
        c_ref[...] = jnp.zeros_like(c_ref)

    @pl.when((s == 0) | ((s >= 2) & (e_out == 0)))
    def _():
        acc_ref[...] = jnp.zeros_like(acc_ref)

    def token_group(grp):
        cols = slice(grp * EXP_TOK_GROUP, (grp + 1) * EXP_TOK_GROUP)
        acc_ref[:, cols] += _dot(vt_ref[...], c_ref[:, cols])
        for ib in range(EXP_TILE // PEER_NKEYS):
            i = e_gate * (EXP_TILE // PEER_NKEYS) + ib
            rows = slice(ib * PEER_NKEYS, (ib + 1) * PEER_NKEYS)
            rank_rows = [pa_ref[pl.ds(h * SEL_ROWS_PER_HEAD + i, 1), :] for h in range(PEER_HEADS)]
            e1_rows = [pa_ref[pl.ds(h * SEL_ROWS_PER_HEAD + PEER_NKEYS + i, 1), :] for h in range(PEER_HEADS)]
            for lc in range(EXP_TOK_GROUP // LANES):
                first = grp * EXP_TOK_GROUP + lc * LANES
                lanes = slice(first, first + LANES)
                a = a_ref[rows, lanes]
                act = (a + a * lax.erf(a)).astype(BF16).reshape(tile3)
                w = jnp.zeros(tile3, BF16)
                for h in range(PEER_HEADS):
                    base = h * SEL_ROWS_PER_HEAD
                    rank_i = jnp.broadcast_to(rank_rows[h][:, lanes], (BF16_ROWS, LANES)).astype(BF16)
                    e1_i = jnp.broadcast_to(e1_rows[h][:, lanes], (BF16_ROWS, LANES)).astype(BF16)
                    n2 = pb_ref[base:base + PEER_NKEYS, lanes].reshape(tile3)
                    e2 = pb_ref[base + PEER_NKEYS:base + 2 * PEER_NKEYS, lanes].reshape(tile3)
                    w = w + jnp.where(rank_i[None] < n2, e1_i[None] * e2, jnp.zeros((), BF16))
                c_ref[rows, lanes] = (act * w).reshape(PEER_NKEYS, LANES)
        a_ref[:, cols] = _dot(u_ref[...], h2t_ref[:, cols])

    for grp in range(EXP_TOK_TILE // EXP_TOK_GROUP):
        token_group(grp)

    @pl.when((s >= 2) & (e_out == N_EXP_TILES - 1))
    def _():
        mod = mod_ref[0]
        gate2 = mod[:, 5 * D_MODEL:6 * D_MODEL]
        y = x1_ref[...] + gate2 * acc_ref[...].T
        ms = jnp.mean(y * y, axis=-1, keepdims=True)
        y = (y * lax.rsqrt(ms + EPS)) * gfin_ref[...]
        is_ctx = jnp.maximum(s - 2, 0) // N_EXP_TILES < N_CTX_TOK // EXP_TOK_TILE

        @pl.when(is_ctx)
        def _():
            yp_ref[...] = y

        @pl.when(jnp.logical_not(is_ctx))
        def _():
            ys_ref[...] = y


def _experts(h2t, pa, pb, u_bf, vt_bf, x1, mod3, gfin):
    tm = EXP_TOK_TILE
    n_pe = PEER_HEADS * SEL_ROWS_PER_HEAD
    item = lambda s, lag: jnp.clip(s - lag, 0, N_EXP_WORK - 1)
    tok = lambda s, lag: item(s, lag) // N_EXP_TILES
    exp = lambda s, lag: item(s, lag) % N_EXP_TILES
    n_ctx_tiles = N_CTX_TOK // tm
    return pl.pallas_call(
        _experts_kernel,
        grid=(N_EXP_STEPS,),
        in_specs=[
            pl.BlockSpec((D_MODEL, tm), lambda s: (0, tok(s, 0))),
            pl.BlockSpec((n_pe, tm), lambda s: (0, tok(s, 1))),
            pl.BlockSpec((n_pe, tm), lambda s: (0, tok(s, 1))),
            pl.BlockSpec((EXP_TILE, D_MODEL), lambda s: (exp(s, 0), 0)),
            pl.BlockSpec((D_MODEL, EXP_TILE), lambda s: (0, exp(s, 2))),
            pl.BlockSpec((tm, D_MODEL), lambda s: (tok(s, 2), 0)),
            pl.BlockSpec((1, 1, 6 * D_MODEL), lambda s: (_mod_row(tok(s, 2), tm), 0, 0)),
            pl.BlockSpec((1, D_MODEL), lambda s: (0, 0)),
        ],
        out_specs=[pl.BlockSpec((tm, D_MODEL), lambda s: (jnp.minimum(tok(s, 2), n_ctx_tiles - 1), 0)),
                   pl.BlockSpec((tm, D_MODEL), lambda s: (jnp.maximum(tok(s, 2) - n_ctx_tiles, 0), 0))],
        out_shape=[jax.ShapeDtypeStruct((N_CTX_TOK, D_MODEL), F32), jax.ShapeDtypeStruct((N_LAT_TOK, D_MODEL), F32)],
        scratch_shapes=[pltpu.VMEM((D_MODEL, tm), F32), pltpu.VMEM((EXP_TILE, tm), F32),
                        pltpu.VMEM((EXP_TILE, tm), BF16)],
        compiler_params=_params("arbitrary"),
        name="experts",
    )(h2t, pa, pb, u_bf, vt_bf, x1, mod3, gfin)


def kernel(x_prompt, x_sample, c, state_gla_fwd, state_gla_bwd, c_ctx, norm1_g, w_mod, b_mod, w_in, w_af_up, b_af,
           w_ab_up, b_ab, gla_norm_g, conv_dw_w, conv_dw_b, conv_ln_g, conv_ln_b, conv_pw2, w_out, norm2_g, peer_wq,
           peer_subkeys, peer_u, peer_v, final_norm_g):
    depth = w_in.shape[0]
    assert depth == 1
    l = 0
    xp = x_prompt.reshape(N_CTX_TOK, D_MODEL)
    xs = x_sample.reshape(N_LAT_TOK, D_MODEL)

    c_all = jnp.zeros((N_MOD_PAD, D_MODEL), F32).at[0].set(c_ctx).at[1:N_MOD_ROWS].set(c)
    w = w_in[l]
    w_main = jnp.concatenate([w[:, 0:1536], w[:, 1568:2592]], axis=1).astype(BF16)
    w_ab = jnp.zeros((D_MODEL, LANES), F32).at[:, 0:2 * GLA_RANK].set(w[:, 1536:1568]).astype(BF16)
    w_up = jnp.zeros((LANES, 2 * GLA_DK), F32)
    w_up = w_up.at[0:GLA_RANK, 0:GLA_DK].set(w_af_up[l]).at[GLA_RANK:2 * GLA_RANK, GLA_DK:].set(w_ab_up[l]).astype(BF16)
    b_up = jnp.concatenate([b_af[l], b_ab[l]])[None, :]
    conv_w = jnp.zeros((CONV_K + 1, CONV_CH), F32).at[0:CONV_K].set(conv_dw_w[l])
    wo1 = w_out[l][0:GLA_DV].astype(BF16)
    wo2 = w_out[l][GLA_DV:].astype(BF16)
    wqt = peer_wq[l].T.astype(BF16)
    keys = peer_subkeys[l].reshape(2 * PEER_HEADS, PEER_NKEYS, PEER_DQ // 2).astype(BF16)
    u_bf = peer_u[l].astype(BF16)
    vt_bf = peer_v[l].T.astype(BF16)

    mod = _modulation(c_all, w_mod[l], b_mod[l][None, :])
    mod3 = mod[:, None, :]

    q, k, v, gf, gb, sg, u = _mixin(xp, xs, mod3, norm1_g[l][None, :], w_main, w_ab, w_up, b_up)

    gn = gla_norm_g[l][None, :]
    mo0 = jnp.zeros((N_TOK, GLA_DV), BF16)
    mo1, new_f, new_b = _gla(q, k, v, gf, gb, sg, gn, mo0, seq_len=CTX_LEN, n_seq=N_CTX_SEQ, first_block=0,
                             init_states=None, emit_state=True)
    (mo,) = _gla(q, k, v, gf, gb, sg, gn, mo1, seq_len=LAT_LEN, n_seq=N_LAT_SEQ,
                 first_block=N_CTX_TOK // LAT_LEN, init_states=(state_gla_fwd[:, l], state_gla_bwd[:, l]),
                 emit_state=False)

    conv_args = (conv_w, conv_dw_b[l][None, :], conv_ln_g[l][None, :], conv_ln_b[l][None, :], conv_pw2[l].astype(BF16))
    uu0 = jnp.zeros((N_TOK, CONV_CH), BF16)
    uu1 = _conv(u, *conv_args, uu0, seq_len=CTX_LEN, n_seq=N_CTX_SEQ, first_block=0, latent=False)
    uu = _conv(u, *conv_args, uu1, seq_len=LAT_LEN, n_seq=N_LAT_SEQ, first_block=N_CTX_TOK // LAT_LEN, latent=True)

    x1, h2t, st = _mid(mo, uu, xp, xs, mod3, norm2_g[l][None, :], wo1, wo2, wqt, keys)
    pa, pb = _select(st)
    yp, ys = _experts(h2t, pa, pb, u_bf, vt_bf, x1, mod3, final_norm_g[None, :])

    y_prompt = yp.reshape(N_CTX_SEQ, CTX_LEN, D_MODEL)
    y_sample = ys.reshape(N_LAT_SEQ, LAT_LEN, D_MODEL)
    return (y_prompt, y_sample, new_f[:, None], new_b[:, None])
```
